```python
import jax, jax.numpy as jnp
from jax import lax
import numpy as np

D_MODEL = 1024
BATCH = 8
SEQ = 4096
DEPTH = 4

N_MIXERS = 2
CHUNK = 128
GMLP_WIDTH = D_MODEL
GMLP_GROUPS = 8
GMLP_GROUP_DIM = GMLP_WIDTH // GMLP_GROUPS
RWKV_HEAD = 64
RWKV_HEADS = D_MODEL // RWKV_HEAD
DECAY_LORA = 64
AAA_LORA = 64
MV_LORA = 32
GATE_LORA = 160
D_FF = 4 * D_MODEL
N_A = (DEPTH + 1) // 2
N_B = DEPTH // 2
N_VRES = max(N_B - 1, 0)
ALPHA = (2.0 * DEPTH) ** 0.25
BETA = (8.0 * DEPTH) ** -0.25
LN_EPS = 1e-5
GN_EPS = 64e-5

kernel_name = 'hybrid_gmlp_rwkv7_deepnorm_trunk'


def _layer_norm(x, g, b, eps):
    xf = x.astype(jnp.float32)
    mean = jnp.mean(xf, axis=-1, keepdims=True)
    var = jnp.mean(jnp.square(xf - mean), axis=-1, keepdims=True)
    y = (xf - mean) * lax.rsqrt(var + eps)
    return (y * g.astype(jnp.float32) + b.astype(jnp.float32)).astype(x.dtype)


def _gmlp_mixer(x, w_in, vn_g, vn_b, ws, bs, w_out):
    B, S, _ = x.shape
    z = jax.nn.gelu(x @ w_in, approximate=False)
    u, v = jnp.split(z, 2, axis=-1)
    v = _layer_norm(v, vn_g, vn_b, LN_EPS)
    v = v.reshape(B, S // CHUNK, CHUNK, GMLP_GROUPS, GMLP_GROUP_DIM)
    causal = jnp.tril(jnp.ones((CHUNK, CHUNK), dtype=bool))
    ws_c = jnp.where(causal[None], ws, jnp.zeros_like(ws)).astype(v.dtype)
    mixed = jnp.einsum('gts,bcsgd->bctgd', ws_c, v) + bs.T[None, None, :, :, None]
    gated = u * mixed.reshape(B, S, GMLP_WIDTH)
    return gated @ w_out


def _token_shift(x):
    return jnp.pad(x, ((0, 0), (1, 0), (0, 0)))[:, :-1, :]


def _wkv7(r, decay, k, v, kk, b):
    xs = tuple(jnp.moveaxis(t.astype(jnp.float32), 1, 0) for t in (r, decay, k, v, kk, b))
    Bsz, _, H, N = r.shape

    def step(state, inp):
        r_t, w_t, k_t, v_t, kk_t, b_t = inp
        sa = jnp.einsum('bhij,bhj->bhi', state, -kk_t)
        state = (state * w_t[:, :, None, :] + sa[..., None] * b_t[:, :, None, :]
                 + v_t[..., None] * k_t[:, :, None, :])
        y = jnp.einsum('bhij,bhj->bhi', state, r_t)
        return state, y

    s0 = jnp.zeros((Bsz, H, N, N), jnp.float32)
    _, ys = lax.scan(step, s0, xs)
    return jnp.moveaxis(ys, 0, 1)


def _rwkv7_mixer(x, v_first, mu, w_rkv, w0, w1, w2, a0, a1, a2, g1, g2,
                 k_k, k_a, r_k, gn_g, gn_b, w_o, vres):
    B, S, D = x.shape
    H, N = RWKV_HEADS, RWKV_HEAD
    xx = _token_shift(x) - x
    xr, xw, xk, xv, xa, xg = (x + xx * mu[i] for i in range(6))
    r = xr @ w_rkv[0]
    k = xk @ w_rkv[1]
    v = xv @ w_rkv[2]
    w_log = -jax.nn.softplus(-(w0 + jnp.tanh(xw @ w1) @ w2)) - 0.5
    decay = jnp.exp(-jnp.exp(w_log.astype(jnp.float32)))
    a = jax.nn.sigmoid(a0 + (xa @ a1) @ a2)
    g = jax.nn.sigmoid(xg @ g1) @ g2
    if vres is None:
        v_first = v
    else:
        v0, v1, v2 = vres
        v = v + (v_first - v) * jax.nn.sigmoid(v0 + (xv @ v1) @ v2)
    kk = (k * k_k).reshape(B, S, H, N).astype(jnp.float32)
    kk = kk / jnp.maximum(jnp.sqrt(jnp.sum(kk * kk, axis=-1, keepdims=True)), 1e-12)
    k = k * (1.0 + (a - 1.0) * k_a)
    rh = r.reshape(B, S, H, N)
    kh = k.reshape(B, S, H, N)
    vh = v.reshape(B, S, H, N)
    ah = a.reshape(B, S, H, N).astype(jnp.float32)
    y = _wkv7(rh, decay.reshape(B, S, H, N), kh, vh, kk, kk * ah)
    mean = jnp.mean(y, axis=-1, keepdims=True)
    var = jnp.mean(jnp.square(y - mean), axis=-1, keepdims=True)
    y = ((y - mean) * lax.rsqrt(var + GN_EPS)).reshape(B, S, D)
    y = (y * gn_g.astype(jnp.float32) + gn_b.astype(jnp.float32)).astype(x.dtype)
    bonus = (jnp.sum(rh * kh * r_k, axis=-1, keepdims=True) * vh).reshape(B, S, D)
    return ((y + bonus) * g) @ w_o, v_first


def setup_inputs(seed: int = 0) -> dict:
    key = jax.random.key(seed)
    ks = iter(jax.random.split(key, 40))

    def nrm(shape, scale):
        return jax.random.normal(next(ks), shape, jnp.float32) * scale

    lin = jnp.arange(D_MODEL, dtype=jnp.float32) / (D_MODEL - 1)
    ratio = jnp.arange(N_B, dtype=jnp.float32) / max(N_B - 1, 1)
    decay_speed = -7.0 + 5.0 * lin[None, :] ** (0.85 + ratio[:, None] ** 0.5)
    return {
        'x': nrm((BATCH, SEQ, D_MODEL), 1.0),
        'ln1_g': 1.0 + nrm((DEPTH, D_MODEL), 0.02),
        'ln1_b': nrm((DEPTH, D_MODEL), 0.02),
        'ln2_g': 1.0 + nrm((DEPTH, D_MODEL), 0.02),
        'ln2_b': nrm((DEPTH, D_MODEL), 0.02),
        'ffn_w1': nrm((DEPTH, D_MODEL, D_FF), D_MODEL ** -0.5),
        'ffn_w2': nrm((DEPTH, D_FF, D_MODEL), D_FF ** -0.5 * BETA),
        'gm_w_in': nrm((N_A, D_MODEL, 2 * GMLP_WIDTH), D_MODEL ** -0.5),
        'gm_vn_g': 1.0 + nrm((N_A, GMLP_WIDTH), 0.02),
        'gm_vn_b': nrm((N_A, GMLP_WIDTH), 0.02),
        'gm_ws': nrm((N_A, GMLP_GROUPS, CHUNK, CHUNK), CHUNK ** -0.5),
        'gm_bs': 1.0 + nrm((N_A, GMLP_GROUPS, CHUNK), 0.1),
        'gm_w_out': nrm((N_A, GMLP_WIDTH, D_MODEL), GMLP_WIDTH ** -0.5 * BETA),
        'rw_mu': jax.random.uniform(next(ks), (N_B, 6, D_MODEL), jnp.float32),
        'rw_w_rkv': nrm((N_B, 3, D_MODEL, D_MODEL), D_MODEL ** -0.5),
        'rw_w0': decay_speed + 0.5 + nrm((N_B, D_MODEL), 0.1),
        'rw_w1': nrm((N_B, D_MODEL, DECAY_LORA), D_MODEL ** -0.5),
        'rw_w2': nrm((N_B, DECAY_LORA, D_MODEL), 0.1 * DECAY_LORA ** -0.5),
        'rw_a0': nrm((N_B, D_MODEL), 0.1),
        'rw_a1': nrm((N_B, D_MODEL, AAA_LORA), D_MODEL ** -0.5),
        'rw_a2': nrm((N_B, AAA_LORA, D_MODEL), 0.1 * AAA_LORA ** -0.5),
        'rw_g1': nrm((N_B, D_MODEL, GATE_LORA), D_MODEL ** -0.5),
        'rw_g2': nrm((N_B, GATE_LORA, D_MODEL), GATE_LORA ** -0.5),
        'rw_k_k': 0.85 + nrm((N_B, D_MODEL), 0.05),
        'rw_k_a': 1.0 + nrm((N_B, D_MODEL), 0.05),
        'rw_r_k': -0.04 + nrm((N_B, RWKV_HEADS, RWKV_HEAD), 0.1),
        'rw_gn_g': 1.0 + nrm((N_B, D_MODEL), 0.02),
        'rw_gn_b': nrm((N_B, D_MODEL), 0.02),
        'rw_w_o': nrm((N_B, D_MODEL, D_MODEL), D_MODEL ** -0.5 * BETA),
        'rw_v0': 1.0 + nrm((N_VRES, D_MODEL), 0.1),
        'rw_v1': nrm((N_VRES, D_MODEL, MV_LORA), D_MODEL ** -0.5),
        'rw_v2': nrm((N_VRES, MV_LORA, D_MODEL), 0.1 * MV_LORA ** -0.5),
    }


def reference(x, ln1_g, ln1_b, ln2_g, ln2_b, ffn_w1, ffn_w2,
              gm_w_in, gm_vn_g, gm_vn_b, gm_ws, gm_bs, gm_w_out,
              rw_mu, rw_w_rkv, rw_w0, rw_w1, rw_w2, rw_a0, rw_a1, rw_a2,
              rw_g1, rw_g2, rw_k_k, rw_k_a, rw_r_k, rw_gn_g, rw_gn_b, rw_w_o,
              rw_v0, rw_v1, rw_v2):
    h = x
    v_first = None
    for i in range(DEPTH):
        j = i // N_MIXERS
        if i % N_MIXERS == 0:
            mix = _gmlp_mixer(h, gm_w_in[j], gm_vn_g[j], gm_vn_b[j], gm_ws[j],
                              gm_bs[j], gm_w_out[j])
        else:
            vres = None if j == 0 else (rw_v0[j - 1], rw_v1[j - 1], rw_v2[j - 1])
            mix, v_first = _rwkv7_mixer(h, v_first, rw_mu[j], rw_w_rkv[j], rw_w0[j],
                                        rw_w1[j], rw_w2[j], rw_a0[j], rw_a1[j], rw_a2[j],
                                        rw_g1[j], rw_g2[j], rw_k_k[j], rw_k_a[j],
                                        rw_r_k[j], rw_gn_g[j], rw_gn_b[j], rw_w_o[j], vres)
        h = _layer_norm(ALPHA * h + mix, ln1_g[i], ln1_b[i], LN_EPS)
        f = jnp.square(jax.nn.relu(h @ ffn_w1[i])) @ ffn_w2[i]
        h = _layer_norm(ALPHA * h + f, ln2_g[i], ln2_b[i], LN_EPS)
    return h
```

```python
import functools
import math

import jax
import jax.numpy as jnp
from jax import lax
from jax.experimental import pallas as pl
from jax.experimental.pallas import tpu as pltpu

F32 = jnp.float32
BF16 = jnp.bfloat16

GMLP_CHUNK = 128
GMLP_GROUPS = 8
RWKV_HEAD = 64
LN_EPS = 1e-5
GN_EPS = 64e-5

LANES = 128
WKV_CHUNK = 64
VMEM_LIMIT = 56 * 1024 * 1024


def _bdot(a, b):
    return jnp.dot(a.astype(BF16), b.astype(BF16), preferred_element_type=F32)


def _bdot_nt(a, b):
    return lax.dot_general(a.astype(BF16), b.astype(BF16), (((1,), (1,)), ((), ())),
                           preferred_element_type=F32)


def _split3(x):
    hi = x.astype(BF16)
    r1 = x - hi.astype(F32)
    mid = r1.astype(BF16)
    lo = (r1 - mid.astype(F32)).astype(BF16)
    return hi, mid, lo


def _dot_exact_lhs(a_bf16, x):
    hi, mid, lo = _split3(x)
    out = jnp.dot(a_bf16, hi, preferred_element_type=F32)
    out += jnp.dot(a_bf16, mid, preferred_element_type=F32)
    out += jnp.dot(a_bf16, lo, preferred_element_type=F32)
    return out


def _dot_exact_rhs(x, b_bf16):
    hi, mid, lo = _split3(x)
    out = jnp.dot(hi, b_bf16, preferred_element_type=F32)
    out += jnp.dot(mid, b_bf16, preferred_element_type=F32)
    out += jnp.dot(lo, b_bf16, preferred_element_type=F32)
    return out


def _layer_norm(x, g, b, eps):
    mean = jnp.mean(x, axis=-1, keepdims=True)
    xc = x - mean
    var = jnp.mean(xc * xc, axis=-1, keepdims=True)
    return xc * lax.rsqrt(var + eps) * g + b


def _const_spec(shape):
    nd = len(shape)
    return pl.BlockSpec(shape, lambda *_: (0,) * nd, pipeline_mode=pl.Buffered(1))


def _params(*sem):
    return pltpu.CompilerParams(dimension_semantics=sem, vmem_limit_bytes=VMEM_LIMIT)


def _ffn_kernel(h_ref, w1_ref, w2_ref, g_ref, b_ref, o_ref, *, alpha, ff_block):
    h = h_ref[...]
    hb = h.astype(BF16)
    d_ff = w1_ref.shape[1]
    acc = alpha * h
    for j in range(d_ff // ff_block):
        a = jnp.dot(hb, w1_ref[:, j * ff_block:(j + 1) * ff_block], preferred_element_type=F32)
        a = jnp.maximum(a, 0.0)
        a = (a * a).astype(BF16)
        acc += jnp.dot(a, w2_ref[j * ff_block:(j + 1) * ff_block, :], preferred_element_type=F32)
    o_ref[...] = _layer_norm(acc, g_ref[...], b_ref[...], LN_EPS)


def _ffn(h, w1, w2, g, b, alpha, tm=512, ff_block=1024):
    t, d = h.shape
    d_ff = w1.shape[1]
    return pl.pallas_call(
        functools.partial(_ffn_kernel, alpha=alpha, ff_block=ff_block),
        grid=(t // tm,),
        in_specs=[pl.BlockSpec((tm, d), lambda i: (i, 0)),
                  _const_spec((d, d_ff)), _const_spec((d_ff, d)),
                  _const_spec((1, d)), _const_spec((1, d))],
        out_specs=pl.BlockSpec((tm, d), lambda i: (i, 0)),
        out_shape=jax.ShapeDtypeStruct((t, d), F32),
        compiler_params=_params("parallel"),
        name="ffn_ln",
    )(h, w1, w2, g, b)


def _gmlp_kernel(x_ref, win_ref, vng_ref, vnb_ref, ws_ref, bs_ref, wout_ref, g_ref, b_ref, o_ref,
                 *, alpha):
    x = x_ref[...]
    tm = x.shape[0]
    gw = wout_ref.shape[0]
    gd = gw // GMLP_GROUPS
    z = jnp.dot(x.astype(BF16), win_ref[...], preferred_element_type=F32)
    z = 0.5 * z * (1.0 + lax.erf(z * math.sqrt(0.5)))
    u = z[:, :gw]
    v = _layer_norm(z[:, gw:], vng_ref[...], vnb_ref[...], LN_EPS).astype(BF16)
    rows = []
    for c in range(tm // GMLP_CHUNK):
        cols = []
        for g in range(GMLP_GROUPS):
            vb = v[c * GMLP_CHUNK:(c + 1) * GMLP_CHUNK, g * gd:(g + 1) * gd]
            cols.append(jnp.dot(ws_ref[g], vb, preferred_element_type=F32) + bs_ref[g])
        rows.append(jnp.concatenate(cols, axis=1))
    mixed = jnp.concatenate(rows, axis=0)
    mix = jnp.dot((u * mixed).astype(BF16), wout_ref[...], preferred_element_type=F32)
    o_ref[...] = _layer_norm(alpha * x + mix, g_ref[...], b_ref[...], LN_EPS)


def _gmlp(x, w_in, vn_g, vn_b, ws, bs, w_out, g, b, alpha, tm=512):
    t, d = x.shape
    gw = w_out.shape[0]
    return pl.pallas_call(
        functools.partial(_gmlp_kernel, alpha=alpha),
        grid=(t // tm,),
        in_specs=[pl.BlockSpec((tm, d), lambda i: (i, 0)),
                  _const_spec((d, 2 * gw)), _const_spec((1, gw)), _const_spec((1, gw)),
                  _const_spec(ws.shape), _const_spec(bs.shape), _const_spec((gw, d)),
                  _const_spec((1, d)), _const_spec((1, d))],
        out_specs=pl.BlockSpec((tm, d), lambda i: (i, 0)),
        out_shape=jax.ShapeDtypeStruct((t, d), F32),
        compiler_params=_params("parallel"),
        name="gmlp_ln",
    )(x, w_in, vn_g, vn_b, ws, bs, w_out, g, b)


def _rwkv_proj_kernel(*refs, seq_tiles, has_vres):
    if has_vres:
        (x_ref, xp_ref, mu_ref, wr_ref, wk_ref, wv_ref, w0_ref, w1_ref, w2_ref, a0_ref, a1_ref,
         a2_ref, g1_ref, g2_ref, vf_ref, v0_ref, v1_ref, v2_ref,
         r_out, lw_out, k_out, v_out, a_out, g_out) = refs
    else:
        (x_ref, xp_ref, mu_ref, wr_ref, wk_ref, wv_ref, w0_ref, w1_ref, w2_ref, a0_ref, a1_ref,
         a2_ref, g1_ref, g2_ref,
         r_out, lw_out, k_out, v_out, a_out, g_out) = refs
    x = x_ref[...]
    tm = x.shape[0]
    not_first = (pl.program_id(0) % seq_tiles != 0).astype(F32)
    prev = xp_ref[7:8, :] * not_first
    row = lax.broadcasted_iota(jnp.int32, x.shape, 0)
    shifted = jnp.where(row == 0, prev, pltpu.roll(x, shift=1, axis=0))
    xx = shifted - x
    mixed = [x + xx * mu_ref[i:i + 1, :] for i in range(6)]
    xr, xw, xk, xv, xa, xg = mixed
    r = _bdot(xr, wr_ref[...])
    k = _bdot(xk, wk_ref[...])
    v = _bdot(xv, wv_ref[...])
    w_in = w0_ref[...] + _bdot(jnp.tanh(_bdot(xw, w1_ref[...])), w2_ref[...])
    w_log = -jax.nn.softplus(-w_in) - 0.5
    lw = -jnp.exp(w_log)
    a = jax.nn.sigmoid(a0_ref[...] + _bdot(_bdot(xa, a1_ref[...]), a2_ref[...]))
    g = _bdot(jax.nn.sigmoid(_bdot(xg, g1_ref[...])), g2_ref[...])
    if has_vres:
        vf = jnp.concatenate([vf_ref[p] for p in range(vf_ref.shape[0])], axis=1)
        gate = jax.nn.sigmoid(v0_ref[...] + _bdot(_bdot(xv, v1_ref[...]), v2_ref[...]))
        v = v + (vf - v) * gate
    for out, val in ((r_out, r), (lw_out, lw), (k_out, k), (v_out, v), (a_out, a), (g_out, g)):
        for p in range(out.shape[0]):
            out[p] = val[:, p * LANES:(p + 1) * LANES]


def _rwkv_proj(h, seq, mu, wr, wk, wv, w0, w1, w2, a0, a1, a2, g1, g2, vres, tm=256):
    t, d = h.shape
    pairs = d // LANES
    has_vres = vres is not None
    row_spec = pl.BlockSpec((tm, d), lambda i: (i, 0))
    prev_spec = pl.BlockSpec((8, d), lambda i: (jnp.maximum(i * (tm // 8) - 1, 0), 0))
    pm_spec = pl.BlockSpec((pairs, tm, LANES), lambda i: (0, i, 0))
    args = [h, h, mu, wr, wk, wv, w0, w1, w2, a0, a1, a2, g1, g2]
    in_specs = [row_spec, prev_spec] + [_const_spec(a.shape) for a in args[2:]]
    if has_vres:
        v_first, v0, v1, v2 = vres
        args += [v_first, v0, v1, v2]
        in_specs += [pm_spec, _const_spec(v0.shape), _const_spec(v1.shape), _const_spec(v2.shape)]
    pm_shape = jax.ShapeDtypeStruct((pairs, t, LANES), F32)
    return pl.pallas_call(
        functools.partial(_rwkv_proj_kernel, seq_tiles=seq // tm, has_vres=has_vres),
        grid=(t // tm,),
        in_specs=in_specs,
        out_specs=[pm_spec] * 6,
        out_shape=[pm_shape] * 6,
        compiler_params=_params("parallel"),
        name="rwkv_proj",
    )(*args)


def _block_diag(x, lo_mask):
    zero = jnp.zeros_like(x)
    return jnp.concatenate([jnp.where(lo_mask, x, zero), jnp.where(lo_mask, zero, x)], axis=0)


def _wkv_kernel(r_ref, lw_ref, k_ref, v_ref, a_ref, kk_ref, ka_ref, rk_ref, gg_ref, gb_ref,
                y_ref, h_ref):
    ts = r_ref.shape[1]
    n_chunks = ts // WKV_CHUNK
    L = WKV_CHUNK

    @pl.when(pl.program_id(2) == 0)
    def _():
        h_ref[...] = jnp.zeros_like(h_ref)

    r = r_ref[0]
    lw = lw_ref[0]
    k = k_ref[0]
    v = v_ref[0]
    a = a_ref[0]

    li = lax.broadcasted_iota(jnp.int32, (L, LANES), 0)
    lj = lax.broadcasted_iota(jnp.int32, (L, LANES), 1)
    lo_mask = lj < RWKV_HEAD
    ljm = jnp.where(lo_mask, lj, lj - RWKV_HEAD)
    strict = li > ljm
    incl = li >= ljm
    eye_pair = (li == ljm).astype(F32)
    qi = lax.broadcasted_iota(jnp.int32, (LANES, LANES), 0)
    qj = lax.broadcasted_iota(jnp.int32, (LANES, LANES), 1)
    same_head = (qi // RWKV_HEAD) == (qj // RWKV_HEAD)
    head_ones = same_head.astype(BF16)
    ti = lax.broadcasted_iota(jnp.int32, (ts, ts), 0)
    tj = lax.broadcasted_iota(jnp.int32, (ts, ts), 1)
    tri = ((ti >= tj) & (ti // L == tj // L)).astype(BF16)

    kraw = k * kk_ref[0]
    ss = _dot_exact_rhs(kraw * kraw, head_ones)
    kappa = kraw / jnp.maximum(jnp.sqrt(ss), 1e-12)
    bvec = kappa * a
    k2 = k * (1.0 + (a - 1.0) * ka_ref[0])

    cs = _dot_exact_lhs(tri, lw)
    g_incl = jnp.exp(cs)
    g_excl = jnp.exp(cs - lw)
    g_inv = jnp.exp(-cs)
    rt_all = r * g_incl
    kt_all = kappa * g_excl
    bt_all = bvec * g_inv
    kk_all = k2 * g_inv

    y_chunks = []
    stage = []
    for c in range(n_chunks):
        sl = slice(c * L, (c + 1) * L)
        rt, kt, bt, kk_, vv = rt_all[sl], kt_all[sl], bt_all[sl], kk_all[sl], v[sl]
        g_last = g_incl[(c + 1) * L - 1:(c + 1) * L, :]
        bd_b = _block_diag(bt.astype(BF16), lo_mask)
        bd_k = _block_diag(kk_.astype(BF16), lo_mask)
        lhs = jnp.concatenate([kt, rt], axis=0).astype(BF16)
        rhs = jnp.concatenate([bd_b, bd_k], axis=0)
        att = lax.dot_general(lhs, rhs, (((1,), (1,)), ((), ())), preferred_element_type=F32)
        a_ab = jnp.where(strict, att[:L, :LANES], 0.0)
        a_ak = jnp.where(strict, att[:L, LANES:], 0.0)
        a_rb = jnp.where(incl, att[L:, :LANES], 0.0)
        a_rk = jnp.where(incl, att[L:, LANES:], 0.0)
        q = -a_ab
        p = eye_pair + q
        q = jnp.dot(q.astype(BF16), _block_diag(q.astype(BF16), lo_mask),
                    preferred_element_type=F32)
        for _ in range(int(math.log2(L)) - 2):
            bd_q = _block_diag(q.astype(BF16), lo_mask)
            pq = jnp.dot(jnp.concatenate([q, p], axis=0).astype(BF16), bd_q,
                         preferred_element_type=F32)
            q = pq[:L]
            p = p + pq[L:]
        p = p + jnp.dot(p.astype(BF16), _block_diag(q.astype(BF16), lo_mask),
                        preferred_element_type=F32)
        bd_v = _block_diag(vv.astype(BF16), lo_mask)
        av = jnp.dot(a_ak.astype(BF16), bd_v, preferred_element_type=F32)
        tw = jnp.dot(p.astype(BF16),
                     jnp.concatenate([_block_diag(kt.astype(BF16), lo_mask),
                                      _block_diag(av.astype(BF16), lo_mask)], axis=1),
                     preferred_element_type=F32)
        wt = tw[:, :LANES]
        u_loc = -tw[:, LANES:]
        bd_u = _block_diag(u_loc.astype(BF16), lo_mask)
        y_loc = jnp.dot(jnp.concatenate([a_rb, a_rk], axis=1).astype(BF16),
                        jnp.concatenate([bd_u, bd_v], axis=0), preferred_element_type=F32)
        r_hat = rt - jnp.dot(a_rb.astype(BF16), _block_diag(wt.astype(BF16), lo_mask),
                             preferred_element_type=F32)
        bk_end = jnp.concatenate([bt * g_last, kk_ * g_last], axis=0).T.astype(BF16)
        wt_pad = jnp.concatenate([wt, jnp.zeros_like(wt)], axis=0)
        m_corr = jnp.where(same_head, _bdot(bk_end, wt_pad), 0.0)
        g_add = jnp.where(same_head,
                          _bdot(bk_end, jnp.concatenate([u_loc, vv], axis=0)), 0.0)
        g_col = jnp.broadcast_to(g_last, (LANES, LANES)).T
        stage.append((y_loc, r_hat, m_corr, g_add, g_col))

    h = h_ref[...]
    for c in range(n_chunks):
        y_loc, r_hat, m_corr, g_add, g_col = stage[c]
        prod = jnp.dot(jnp.concatenate([r_hat, m_corr], axis=0).astype(BF16), h.astype(BF16),
                       preferred_element_type=F32)
        y_chunks.append(y_loc + prod[:L])
        h = g_col * h - prod[L:] + g_add
    h_ref[...] = h

    y = jnp.concatenate(y_chunks, axis=0)
    inv_n = 1.0 / RWKV_HEAD
    mean = _dot_exact_rhs(y, head_ones) * inv_n
    yc = y - mean
    var = _dot_exact_rhs(yc * yc, head_ones) * inv_n
    yn = yc * lax.rsqrt(var + GN_EPS) * gg_ref[0] + gb_ref[0]
    bonus = _dot_exact_rhs(r * k2 * rk_ref[0], head_ones) * v
    y_ref[0] = yn + bonus


def _wkv(r, lw, k, v, a, k_k, k_a, r_k, gn_g, gn_b, batch, seq, ts=256):
    pairs, t, _ = r.shape
    s_tiles = seq // ts
    tok_spec = pl.BlockSpec((1, ts, LANES), lambda p, b, s: (p, b * s_tiles + s, 0))
    par_spec = pl.BlockSpec((1, 1, LANES), lambda p, b, s: (p, 0, 0))
    return pl.pallas_call(
        _wkv_kernel,
        grid=(pairs, batch, s_tiles),
        in_specs=[tok_spec] * 5 + [par_spec] * 5,
        out_specs=tok_spec,
        out_shape=jax.ShapeDtypeStruct((pairs, t, LANES), F32),
        scratch_shapes=[pltpu.VMEM((LANES, LANES), F32)],
        compiler_params=_params("parallel", "parallel", "arbitrary"),
        name="wkv7_chunked",
    )(r, lw, k, v, a, k_k, k_a, r_k, gn_g, gn_b)


def _rwkv_out_kernel(y_ref, g_ref, h_ref, wo_ref, lg_ref, lb_ref, o_ref, *, alpha):
    pairs = y_ref.shape[0]
    yg = jnp.concatenate([(y_ref[p] * g_ref[p]).astype(BF16) for p in range(pairs)], axis=1)
    mix = jnp.dot(yg, wo_ref[...], preferred_element_type=F32)
    o_ref[...] = _layer_norm(alpha * h_ref[...] + mix, lg_ref[...], lb_ref[...], LN_EPS)


def _rwkv_out(y, g, h, w_o, ln_g, ln_b, alpha, tm=512):
    t, d = h.shape
    pairs = d // LANES
    pm_spec = pl.BlockSpec((pairs, tm, LANES), lambda i: (0, i, 0))
    row_spec = pl.BlockSpec((tm, d), lambda i: (i, 0))
    return pl.pallas_call(
        functools.partial(_rwkv_out_kernel, alpha=alpha),
        grid=(t // tm,),
        in_specs=[pm_spec, pm_spec, row_spec, _const_spec((d, d)), _const_spec((1, d)),
                  _const_spec((1, d))],
        out_specs=row_spec,
        out_shape=jax.ShapeDtypeStruct((t, d), F32),
        compiler_params=_params("parallel"),
        name="rwkv_out_ln",
    )(y, g, h, w_o, ln_g, ln_b)


def _pair_major(vec):
    return vec.reshape(-1, 1, LANES)


def kernel(x, ln1_g, ln1_b, ln2_g, ln2_b, ffn_w1, ffn_w2, gm_w_in, gm_vn_g, gm_vn_b, gm_ws, gm_bs, gm_w_out, rw_mu, rw_w_rkv, rw_w0, rw_w1, rw_w2, rw_a0, rw_a1, rw_a2, rw_g1, rw_g2, rw_k_k, rw_k_a, rw_r_k, rw_gn_g, rw_gn_b, rw_w_o, rw_v0, rw_v1, rw_v2):
    batch, seq, d = x.shape
    depth = ln1_g.shape[0]
    alpha = (2.0 * depth) ** 0.25
    h = x.reshape(batch * seq, d)
    row = lambda vec: vec.reshape(1, -1)
    causal = jnp.tril(jnp.ones((GMLP_CHUNK, GMLP_CHUNK), dtype=bool))
    v_first = None
    for i in range(depth):
        j = i // 2
        if i % 2 == 0:
            ws = jnp.where(causal[None], gm_ws[j], 0.0).astype(BF16)
            bs = jnp.broadcast_to(gm_bs[j][:, :, None], gm_bs[j].shape + (d // GMLP_GROUPS,))
            h = _gmlp(h, gm_w_in[j].astype(BF16), row(gm_vn_g[j]), row(gm_vn_b[j]), ws, bs,
                      gm_w_out[j].astype(BF16), row(ln1_g[i]), row(ln1_b[i]), alpha)
        else:
            vres = None
            if j > 0:
                vres = (v_first, row(rw_v0[j - 1]), rw_v1[j - 1].astype(BF16),
                        rw_v2[j - 1].astype(BF16))
            r, lw, k, v, a, g = _rwkv_proj(
                h, seq, rw_mu[j], rw_w_rkv[j, 0].astype(BF16), rw_w_rkv[j, 1].astype(BF16),
                rw_w_rkv[j, 2].astype(BF16), row(rw_w0[j]), rw_w1[j].astype(BF16),
                rw_w2[j].astype(BF16), row(rw_a0[j]), rw_a1[j].astype(BF16),
                rw_a2[j].astype(BF16), rw_g1[j].astype(BF16), rw_g2[j].astype(BF16), vres)
            if v_first is None:
                v_first = v
            y = _wkv(r, lw, k, v, a, _pair_major(rw_k_k[j]), _pair_major(rw_k_a[j]),
                     _pair_major(rw_r_k[j].reshape(-1)), _pair_major(rw_gn_g[j]),
                     _pair_major(rw_gn_b[j]), batch, seq)
            h = _rwkv_out(y, g, h, rw_w_o[j].astype(BF16), row(ln1_g[i]), row(ln1_b[i]), alpha)
        h = _ffn(h, ffn_w1[i].astype(BF16), ffn_w2[i].astype(BF16), row(ln2_g[i]), row(ln2_b[i]),
                 alpha)
    return h.reshape(batch, seq, d)
```

```python
import functools
import math

import jax
import jax.numpy as jnp
from jax import lax
from jax.experimental import pallas as pl
from jax.experimental.pallas import tpu as pltpu

F32 = jnp.float32
BF16 = jnp.bfloat16

GMLP_CHUNK = 128
GMLP_GROUPS = 8
RWKV_HEAD = 64
LN_EPS = 1e-5
GN_EPS = 64e-5

LANES = 128
WKV_CHUNK = 64
VMEM_LIMIT = 56 * 1024 * 1024


def _bdot(a, b):
    return jnp.dot(a.astype(BF16), b.astype(BF16), preferred_element_type=F32)


def _bdot_nt(a, b):
    return lax.dot_general(a.astype(BF16), b.astype(BF16), (((1,), (1,)), ((), ())),
                           preferred_element_type=F32)


def _split3(x):
    hi = x.astype(BF16)
    r1 = x - hi.astype(F32)
    mid = r1.astype(BF16)
    lo = (r1 - mid.astype(F32)).astype(BF16)
    return hi, mid, lo


def _dot_exact_lhs(a_bf16, x):
    hi, mid, lo = _split3(x)
    out = jnp.dot(a_bf16, hi, preferred_element_type=F32)
    out += jnp.dot(a_bf16, mid, preferred_element_type=F32)
    out += jnp.dot(a_bf16, lo, preferred_element_type=F32)
    return out


def _dot_exact_rhs(x, b_bf16):
    hi, mid, lo = _split3(x)
    out = jnp.dot(hi, b_bf16, preferred_element_type=F32)
    out += jnp.dot(mid, b_bf16, preferred_element_type=F32)
    out += jnp.dot(lo, b_bf16, preferred_element_type=F32)
    return out


def _layer_norm(x, g, b, eps):
    mean = jnp.mean(x, axis=-1, keepdims=True)
    xc = x - mean
    var = jnp.mean(xc * xc, axis=-1, keepdims=True)
    return xc * lax.rsqrt(var + eps) * g + b


def _const_spec(shape):
    nd = len(shape)
    return pl.BlockSpec(shape, lambda *_: (0,) * nd, pipeline_mode=pl.Buffered(1))


def _params(*sem):
    return pltpu.CompilerParams(dimension_semantics=sem, vmem_limit_bytes=VMEM_LIMIT)


def _ffn_kernel(h_ref, w1_ref, w2_ref, g_ref, b_ref, o_ref, *, alpha, ff_block):
    h = h_ref[...]
    hb = h.astype(BF16)
    d_ff = w1_ref.shape[1]
    acc = alpha * h
    for j in range(d_ff // ff_block):
        a = jnp.dot(hb, w1_ref[:, j * ff_block:(j + 1) * ff_block], preferred_element_type=F32)
        a = jnp.maximum(a, 0.0)
        a = (a * a).astype(BF16)
        acc += jnp.dot(a, w2_ref[j * ff_block:(j + 1) * ff_block, :], preferred_element_type=F32)
    o_ref[...] = _layer_norm(acc, g_ref[...], b_ref[...], LN_EPS)


def _ffn(h, w1, w2, g, b, alpha, tm=512, ff_block=1024):
    t, d = h.shape
    d_ff = w1.shape[1]
    return pl.pallas_call(
        functools.partial(_ffn_kernel, alpha=alpha, ff_block=ff_block),
        grid=(t // tm,),
        in_specs=[pl.BlockSpec((tm, d), lambda i: (i, 0)),
                  _const_spec((d, d_ff)), _const_spec((d_ff, d)),
                  _const_spec((1, d)), _const_spec((1, d))],
        out_specs=pl.BlockSpec((tm, d), lambda i: (i, 0)),
        out_shape=jax.ShapeDtypeStruct((t, d), F32),
        compiler_params=_params("parallel"),
        name="ffn_ln",
    )(h, w1, w2, g, b)


def _gmlp_kernel(x_ref, win_ref, vng_ref, vnb_ref, ws_ref, bs_ref, wout_ref, g_ref, b_ref, o_ref,
                 *, alpha):
    x = x_ref[...]
    tm = x.shape[0]
    gw = wout_ref.shape[0]
    gd = gw // GMLP_GROUPS
    z = jnp.dot(x.astype(BF16), win_ref[...], preferred_element_type=F32)
    z = 0.5 * z * (1.0 + lax.erf(z * math.sqrt(0.5)))
    u = z[:, :gw]
    v = _layer_norm(z[:, gw:], vng_ref[...], vnb_ref[...], LN_EPS).astype(BF16)
    rows = []
    for c in range(tm // GMLP_CHUNK):
        cols = []
        for g in range(GMLP_GROUPS):
            vb = v[c * GMLP_CHUNK:(c + 1) * GMLP_CHUNK, g * gd:(g + 1) * gd]
            cols.append(jnp.dot(ws_ref[g], vb, preferred_element_type=F32) + bs_ref[g])
        rows.append(jnp.concatenate(cols, axis=1))
    mixed = jnp.concatenate(rows, axis=0)
    mix = jnp.dot((u * mixed).astype(BF16), wout_ref[...], preferred_element_type=F32)
    o_ref[...] = _layer_norm(alpha * x + mix, g_ref[...], b_ref[...], LN_EPS)


def _gmlp(x, w_in, vn_g, vn_b, ws, bs, w_out, g, b, alpha, tm=512):
    t, d = x.shape
    gw = w_out.shape[0]
    return pl.pallas_call(
        functools.partial(_gmlp_kernel, alpha=alpha),
        grid=(t // tm,),
        in_specs=[pl.BlockSpec((tm, d), lambda i: (i, 0)),
                  _const_spec((d, 2 * gw)), _const_spec((1, gw)), _const_spec((1, gw)),
                  _const_spec(ws.shape), _const_spec(bs.shape), _const_spec((gw, d)),
                  _const_spec((1, d)), _const_spec((1, d))],
        out_specs=pl.BlockSpec((tm, d), lambda i: (i, 0)),
        out_shape=jax.ShapeDtypeStruct((t, d), F32),
        compiler_params=_params("parallel"),
        name="gmlp_ln",
    )(x, w_in, vn_g, vn_b, ws, bs, w_out, g, b)


def _rwkv_proj_kernel(*refs, seq_tiles, has_vres):
    if has_vres:
        (x_ref, xp_ref, mu_ref, wr_ref, wk_ref, wv_ref, w0_ref, w1_ref, w2_ref, a0_ref, a1_ref,
         a2_ref, g1_ref, g2_ref, vf_ref, v0_ref, v1_ref, v2_ref,
         r_out, lw_out, k_out, v_out, a_out, g_out) = refs
    else:
        (x_ref, xp_ref, mu_ref, wr_ref, wk_ref, wv_ref, w0_ref, w1_ref, w2_ref, a0_ref, a1_ref,
         a2_ref, g1_ref, g2_ref,
         r_out, lw_out, k_out, v_out, a_out, g_out) = refs
    x = x_ref[...]
    tm = x.shape[0]
    not_first = (pl.program_id(0) % seq_tiles != 0).astype(F32)
    prev = xp_ref[7:8, :] * not_first
    row = lax.broadcasted_iota(jnp.int32, x.shape, 0)
    shifted = jnp.where(row == 0, prev, pltpu.roll(x, shift=1, axis=0))
    xx = shifted - x
    mixed = [x + xx * mu_ref[i:i + 1, :] for i in range(6)]
    xr, xw, xk, xv, xa, xg = mixed
    r = _bdot(xr, wr_ref[...])
    k = _bdot(xk, wk_ref[...])
    v = _bdot(xv, wv_ref[...])
    w_in = w0_ref[...] + _bdot(jnp.tanh(_bdot(xw, w1_ref[...])), w2_ref[...])
    w_log = -jax.nn.softplus(-w_in) - 0.5
    lw = -jnp.exp(w_log)
    a = jax.nn.sigmoid(a0_ref[...] + _bdot(_bdot(xa, a1_ref[...]), a2_ref[...]))
    g = _bdot(jax.nn.sigmoid(_bdot(xg, g1_ref[...])), g2_ref[...])
    if has_vres:
        vf = jnp.concatenate([vf_ref[p] for p in range(vf_ref.shape[0])], axis=1)
        gate = jax.nn.sigmoid(v0_ref[...] + _bdot(_bdot(xv, v1_ref[...]), v2_ref[...]))
        v = v + (vf - v) * gate
    for out, val in ((r_out, r), (lw_out, lw), (k_out, k), (v_out, v), (a_out, a), (g_out, g)):
        for p in range(out.shape[0]):
            out[p] = val[:, p * LANES:(p + 1) * LANES]


def _rwkv_proj(h, seq, mu, wr, wk, wv, w0, w1, w2, a0, a1, a2, g1, g2, vres, tm=256):
    t, d = h.shape
    pairs = d // LANES
    has_vres = vres is not None
    row_spec = pl.BlockSpec((tm, d), lambda i: (i, 0))
    prev_spec = pl.BlockSpec((8, d), lambda i: (jnp.maximum(i * (tm // 8) - 1, 0), 0))
    pm_spec = pl.BlockSpec((pairs, tm, LANES), lambda i: (0, i, 0))
    args = [h, h, mu, wr, wk, wv, w0, w1, w2, a0, a1, a2, g1, g2]
    in_specs = [row_spec, prev_spec] + [_const_spec(a.shape) for a in args[2:]]
    if has_vres:
        v_first, v0, v1, v2 = vres
        args += [v_first, v0, v1, v2]
        in_specs += [pm_spec, _const_spec(v0.shape), _const_spec(v1.shape), _const_spec(v2.shape)]
    pm_shape = jax.ShapeDtypeStruct((pairs, t, LANES), F32)
    return pl.pallas_call(
        functools.partial(_rwkv_proj_kernel, seq_tiles=seq // tm, has_vres=has_vres),
        grid=(t // tm,),
        in_specs=in_specs,
        out_specs=[pm_spec] * 6,
        out_shape=[pm_shape] * 6,
        compiler_params=_params("parallel"),
        name="rwkv_proj",
    )(*args)


def _block_diag(x, lo_mask):
    zero = jnp.zeros_like(x)
    return jnp.concatenate([jnp.where(lo_mask, x, zero), jnp.where(lo_mask, zero, x)], axis=0)


def _wkv_kernel(r_ref, lw_ref, k_ref, v_ref, a_ref, kk_ref, ka_ref, rk_ref, gg_ref, gb_ref,
                y_ref, h_ref):
    pairs, ts, _ = r_ref.shape
    rows = pairs * ts
    per_pair = ts // WKV_CHUNK
    n_chunks = pairs * per_pair
    L = WKV_CHUNK

    @pl.when(pl.program_id(1) == 0)
    def _():
        h_ref[...] = jnp.zeros_like(h_ref)

    flat = lambda ref: ref[...].reshape(rows, LANES)
    per_row = lambda ref: jnp.broadcast_to(ref[...], (pairs, ts, LANES)).reshape(rows, LANES)
    r, lw, k, v, a = flat(r_ref), flat(lw_ref), flat(k_ref), flat(v_ref), flat(a_ref)

    li = lax.broadcasted_iota(jnp.int32, (L, LANES), 0)
    lj = lax.broadcasted_iota(jnp.int32, (L, LANES), 1)
    lo_mask = lj < RWKV_HEAD
    ljm = jnp.where(lo_mask, lj, lj - RWKV_HEAD)
    strict = li > ljm
    incl = li >= ljm
    eye_pair = (li == ljm).astype(F32)
    qi = lax.broadcasted_iota(jnp.int32, (LANES, LANES), 0)
    qj = lax.broadcasted_iota(jnp.int32, (LANES, LANES), 1)
    same_head = (qi // RWKV_HEAD) == (qj // RWKV_HEAD)
    head_ones = same_head.astype(BF16)
    tri = ((qi >= qj) & (qi // L == qj // L)).astype(BF16)

    kraw = k * per_row(kk_ref)
    ss = _dot_exact_rhs(kraw * kraw, head_ones)
    kappa = kraw / jnp.maximum(jnp.sqrt(ss), 1e-12)
    bvec = kappa * a
    k2 = k * (1.0 + (a - 1.0) * per_row(ka_ref))

    cs = jnp.concatenate([_dot_exact_lhs(tri, lw[i * LANES:(i + 1) * LANES])
                          for i in range(rows // LANES)], axis=0)
    g_incl = jnp.exp(cs)
    g_excl = jnp.exp(cs - lw)
    g_inv = jnp.exp(-cs)
    rt_all = r * g_incl
    kt_all = kappa * g_excl
    bt_all = bvec * g_inv
    kk_all = k2 * g_inv

    chunks = range(n_chunks)
    bd = lambda x: _block_diag(x.astype(BF16), lo_mask)
    mm = lambda x, w: jnp.dot(x.astype(BF16), w, preferred_element_type=F32)
    rt = [rt_all[c * L:(c + 1) * L] for c in chunks]
    kt = [kt_all[c * L:(c + 1) * L] for c in chunks]
    bt = [bt_all[c * L:(c + 1) * L] for c in chunks]
    kk_ = [kk_all[c * L:(c + 1) * L] for c in chunks]
    vv = [v[c * L:(c + 1) * L] for c in chunks]
    g_last = [g_incl[(c + 1) * L - 1:(c + 1) * L, :] for c in chunks]

    att = [lax.dot_general(jnp.concatenate([kt[c], rt[c]], axis=0).astype(BF16),
                           jnp.concatenate([bd(bt[c]), bd(kk_[c])], axis=0),
                           (((1,), (1,)), ((), ())), preferred_element_type=F32) for c in chunks]
    a_ab = [jnp.where(strict, att[c][:L, :LANES], 0.0) for c in chunks]
    a_ak = [jnp.where(strict, att[c][:L, LANES:], 0.0) for c in chunks]
    a_rb = [jnp.where(incl, att[c][L:, :LANES], 0.0) for c in chunks]
    a_rk = [jnp.where(incl, att[c][L:, LANES:], 0.0) for c in chunks]
    bd_v = [bd(vv[c]) for c in chunks]
    av = [mm(a_ak[c], bd_v[c]) for c in chunks]
    q = [-a_ab[c] for c in chunks]
    p = [eye_pair + q[c] for c in chunks]
    q = [mm(q[c], bd(q[c])) for c in chunks]
    for _ in range(int(math.log2(L)) - 2):
        pq = [mm(jnp.concatenate([q[c], p[c]], axis=0), bd(q[c])) for c in chunks]
        q = [pq[c][:L] for c in chunks]
        p = [p[c] + pq[c][L:] for c in chunks]
    p = [p[c] + mm(p[c], bd(q[c])) for c in chunks]
    tw = [mm(p[c], jnp.concatenate([bd(kt[c]), bd(av[c])], axis=1)) for c in chunks]
    wt = [tw[c][:, :LANES] for c in chunks]
    u_loc = [-tw[c][:, LANES:] for c in chunks]
    y_loc = [mm(jnp.concatenate([a_rb[c], a_rk[c]], axis=1),
                jnp.concatenate([bd(u_loc[c]), bd_v[c]], axis=0)) for c in chunks]
    r_hat = [rt[c] - mm(a_rb[c], bd(wt[c])) for c in chunks]
    bk_end = [jnp.concatenate([bt[c] * g_last[c], kk_[c] * g_last[c]], axis=0).T for c in chunks]
    m_corr = [jnp.where(same_head,
                        mm(bk_end[c], jnp.concatenate([wt[c], jnp.zeros_like(wt[c])],
                                                      axis=0).astype(BF16)), 0.0) for c in chunks]
    g_add = [jnp.where(same_head,
                       mm(bk_end[c], jnp.concatenate([u_loc[c], vv[c]], axis=0).astype(BF16)), 0.0)
             for c in chunks]
    g_col = [jnp.broadcast_to(g_last[c], (LANES, LANES)).T for c in chunks]

    hs = [h_ref[p] for p in range(pairs)]
    y_chunks = [None] * n_chunks
    for j in range(per_pair):
        for p in range(pairs):
            c = p * per_pair + j
            prod = mm(jnp.concatenate([r_hat[c], m_corr[c]], axis=0), hs[p].astype(BF16))
            y_chunks[c] = y_loc[c] + prod[:L]
            hs[p] = g_col[c] * hs[p] - prod[L:] + g_add[c]
    for p in range(pairs):
        h_ref[p] = hs[p]

    y = jnp.concatenate(y_chunks, axis=0)
    inv_n = 1.0 / RWKV_HEAD
    mean = _dot_exact_rhs(y, head_ones) * inv_n
    yc = y - mean
    var = _dot_exact_rhs(yc * yc, head_ones) * inv_n
    yn = yc * lax.rsqrt(var + GN_EPS) * per_row(gg_ref) + per_row(gb_ref)
    bonus = _dot_exact_rhs(r * k2 * per_row(rk_ref), head_ones) * v
    y_ref[...] = (yn + bonus).reshape(pairs, ts, LANES)


def _wkv(r, lw, k, v, a, k_k, k_a, r_k, gn_g, gn_b, batch, seq, ts=128):
    pairs, t, _ = r.shape
    s_tiles = seq // ts
    tok_spec = pl.BlockSpec((pairs, ts, LANES), lambda b, s: (0, b * s_tiles + s, 0))
    return pl.pallas_call(
        _wkv_kernel,
        grid=(batch, s_tiles),
        in_specs=[tok_spec] * 5 + [_const_spec((pairs, 1, LANES))] * 5,
        out_specs=tok_spec,
        out_shape=jax.ShapeDtypeStruct((pairs, t, LANES), F32),
        scratch_shapes=[pltpu.VMEM((pairs, LANES, LANES), F32)],
        compiler_params=_params("parallel", "arbitrary"),
        name="wkv7_chunked",
    )(r, lw, k, v, a, k_k, k_a, r_k, gn_g, gn_b)


def _rwkv_out_kernel(y_ref, g_ref, h_ref, wo_ref, lg_ref, lb_ref, o_ref, *, alpha):
    pairs = y_ref.shape[0]
    yg = jnp.concatenate([(y_ref[p] * g_ref[p]).astype(BF16) for p in range(pairs)], axis=1)
    mix = jnp.dot(yg, wo_ref[...], preferred_element_type=F32)
    o_ref[...] = _layer_norm(alpha * h_ref[...] + mix, lg_ref[...], lb_ref[...], LN_EPS)


def _rwkv_out(y, g, h, w_o, ln_g, ln_b, alpha, tm=512):
    t, d = h.shape
    pairs = d // LANES
    pm_spec = pl.BlockSpec((pairs, tm, LANES), lambda i: (0, i, 0))
    row_spec = pl.BlockSpec((tm, d), lambda i: (i, 0))
    return pl.pallas_call(
        functools.partial(_rwkv_out_kernel, alpha=alpha),
        grid=(t // tm,),
        in_specs=[pm_spec, pm_spec, row_spec, _const_spec((d, d)), _const_spec((1, d)),
                  _const_spec((1, d))],
        out_specs=row_spec,
        out_shape=jax.ShapeDtypeStruct((t, d), F32),
        compiler_params=_params("parallel"),
        name="rwkv_out_ln",
    )(y, g, h, w_o, ln_g, ln_b)


def _pair_major(vec):
    return vec.reshape(-1, 1, LANES)


def kernel(x, ln1_g, ln1_b, ln2_g, ln2_b, ffn_w1, ffn_w2, gm_w_in, gm_vn_g, gm_vn_b, gm_ws, gm_bs, gm_w_out, rw_mu, rw_w_rkv, rw_w0, rw_w1, rw_w2, rw_a0, rw_a1, rw_a2, rw_g1, rw_g2, rw_k_k, rw_k_a, rw_r_k, rw_gn_g, rw_gn_b, rw_w_o, rw_v0, rw_v1, rw_v2):
    batch, seq, d = x.shape
    depth = ln1_g.shape[0]
    alpha = (2.0 * depth) ** 0.25
    h = x.reshape(batch * seq, d)
    row = lambda vec: vec.reshape(1, -1)
    causal = jnp.tril(jnp.ones((GMLP_CHUNK, GMLP_CHUNK), dtype=bool))
    v_first = None
    for i in range(depth):
        j = i // 2
        if i % 2 == 0:
            ws = jnp.where(causal[None], gm_ws[j], 0.0).astype(BF16)
            bs = jnp.broadcast_to(gm_bs[j][:, :, None], gm_bs[j].shape + (d // GMLP_GROUPS,))
            h = _gmlp(h, gm_w_in[j].astype(BF16), row(gm_vn_g[j]), row(gm_vn_b[j]), ws, bs,
                      gm_w_out[j].astype(BF16), row(ln1_g[i]), row(ln1_b[i]), alpha)
        else:
            vres = None
            if j > 0:
                vres = (v_first, row(rw_v0[j - 1]), rw_v1[j - 1].astype(BF16),
                        rw_v2[j - 1].astype(BF16))
            r, lw, k, v, a, g = _rwkv_proj(
                h, seq, rw_mu[j], rw_w_rkv[j, 0].astype(BF16), rw_w_rkv[j, 1].astype(BF16),
                rw_w_rkv[j, 2].astype(BF16), row(rw_w0[j]), rw_w1[j].astype(BF16),
                rw_w2[j].astype(BF16), row(rw_a0[j]), rw_a1[j].astype(BF16),
                rw_a2[j].astype(BF16), rw_g1[j].astype(BF16), rw_g2[j].astype(BF16), vres)
            if v_first is None:
                v_first = v
            y = _wkv(r, lw, k, v, a, _pair_major(rw_k_k[j]), _pair_major(rw_k_a[j]),
                     _pair_major(rw_r_k[j].reshape(-1)), _pair_major(rw_gn_g[j]),
                     _pair_major(rw_gn_b[j]), batch, seq)
            h = _rwkv_out(y, g, h, rw_w_o[j].astype(BF16), row(ln1_g[i]), row(ln1_b[i]), alpha)
        h = _ffn(h, ffn_w1[i].astype(BF16), ffn_w2[i].astype(BF16), row(ln2_g[i]), row(ln2_b[i]),
                 alpha)
    return h.reshape(batch, seq, d)
```

```python
import functools
import math

import jax
import jax.numpy as jnp
from jax import lax
from jax.experimental import pallas as pl
from jax.experimental.pallas import tpu as pltpu

F32 = jnp.float32
BF16 = jnp.bfloat16

GMLP_CHUNK = 128
GMLP_GROUPS = 8
RWKV_HEAD = 64
LN_EPS = 1e-5
GN_EPS = 64e-5

LANES = 128
MXU_DIM = 256
WKV_CHUNK = 64
VMEM_LIMIT = 56 * 1024 * 1024


def _bdot(a, b):
    return jnp.dot(a.astype(BF16), b.astype(BF16), preferred_element_type=F32)


def _split3(x):
    hi = x.astype(BF16)
    r1 = x - hi.astype(F32)
    mid = r1.astype(BF16)
    lo = (r1 - mid.astype(F32)).astype(BF16)
    return hi, mid, lo


def _dot_exact_lhs(a_bf16, x):
    hi, mid, lo = _split3(x)
    out = jnp.dot(a_bf16, hi, preferred_element_type=F32)
    out += jnp.dot(a_bf16, mid, preferred_element_type=F32)
    out += jnp.dot(a_bf16, lo, preferred_element_type=F32)
    return out


def _dot2_rhs(x, b_bf16):
    hi = x.astype(BF16)
    lo = (x - hi.astype(F32)).astype(BF16)
    return (jnp.dot(hi, b_bf16, preferred_element_type=F32)
            + jnp.dot(lo, b_bf16, preferred_element_type=F32))


def _layer_norm(x, g, b, eps):
    mean = jnp.mean(x, axis=-1, keepdims=True)
    xc = x - mean
    var = jnp.mean(xc * xc, axis=-1, keepdims=True)
    return xc * lax.rsqrt(var + eps) * g + b


def _const_spec(shape):
    nd = len(shape)
    return pl.BlockSpec(shape, lambda *_: (0,) * nd, pipeline_mode=pl.Buffered(1))


def _params(*sem):
    return pltpu.CompilerParams(dimension_semantics=sem, vmem_limit_bytes=VMEM_LIMIT)


def _ffn_kernel(h_ref, w1_ref, w2_ref, g_ref, b_ref, o_ref, *, alpha, ff_block):
    h = h_ref[...]
    hb = h.astype(BF16)
    d_ff = w1_ref.shape[1]
    acc = alpha * h
    for j in range(d_ff // ff_block):
        a = jnp.dot(hb, w1_ref[:, j * ff_block:(j + 1) * ff_block], preferred_element_type=F32)
        a = jnp.maximum(a, 0.0)
        a = (a * a).astype(BF16)
        acc += jnp.dot(a, w2_ref[j * ff_block:(j + 1) * ff_block, :], preferred_element_type=F32)
    o_ref[...] = _layer_norm(acc, g_ref[...], b_ref[...], LN_EPS)


def _ffn(h, w1, w2, g, b, alpha, tm=512, ff_block=1024):
    t, d = h.shape
    d_ff = w1.shape[1]
    return pl.pallas_call(
        functools.partial(_ffn_kernel, alpha=alpha, ff_block=ff_block),
        grid=(t // tm,),
        in_specs=[pl.BlockSpec((tm, d), lambda i: (i, 0)),
                  _const_spec((d, d_ff)), _const_spec((d_ff, d)),
                  _const_spec((1, d)), _const_spec((1, d))],
        out_specs=pl.BlockSpec((tm, d), lambda i: (i, 0)),
        out_shape=jax.ShapeDtypeStruct((t, d), F32),
        compiler_params=_params("parallel"),
        name="ffn_ln",
    )(h, w1, w2, g, b)


def _gmlp_kernel(x_ref, win_ref, vng_ref, vnb_ref, ws_ref, bs_ref, wout_ref, g_ref, b_ref, o_ref,
                 *, alpha, parts):
    tm = x_ref.shape[0]
    gw = wout_ref.shape[0]
    gd = gw // GMLP_GROUPS
    pm = tm // parts
    xs = [x_ref[i * pm:(i + 1) * pm] for i in range(parts)]
    zs = [jnp.dot(xs[i].astype(BF16), win_ref[...], preferred_element_type=F32)
          for i in range(parts)]
    for i in range(parts):
        z = zs[i]
        z = 0.5 * z * (1.0 + lax.erf(z * math.sqrt(0.5)))
        u = z[:, :gw]
        v = _layer_norm(z[:, gw:], vng_ref[...], vnb_ref[...], LN_EPS).astype(BF16)
        rows = []
        for c in range(pm // GMLP_CHUNK):
            cols = []
            for g in range(GMLP_GROUPS):
                vb = v[c * GMLP_CHUNK:(c + 1) * GMLP_CHUNK, g * gd:(g + 1) * gd]
                cols.append(jnp.dot(ws_ref[g], vb, preferred_element_type=F32) + bs_ref[g])
            rows.append(jnp.concatenate(cols, axis=1))
        mixed = jnp.concatenate(rows, axis=0)
        mix = jnp.dot((u * mixed).astype(BF16), wout_ref[...], preferred_element_type=F32)
        o_ref[i * pm:(i + 1) * pm] = _layer_norm(alpha * xs[i] + mix, g_ref[...], b_ref[...],
                                                  LN_EPS)


def _gmlp(x, w_in, vn_g, vn_b, ws, bs, w_out, g, b, alpha, tm=1024, parts=4):
    t, d = x.shape
    gw = w_out.shape[0]
    return pl.pallas_call(
        functools.partial(_gmlp_kernel, alpha=alpha, parts=parts),
        grid=(t // tm,),
        in_specs=[pl.BlockSpec((tm, d), lambda i: (i, 0)),
                  _const_spec((d, 2 * gw)), _const_spec((1, gw)), _const_spec((1, gw)),
                  _const_spec(ws.shape), _const_spec(bs.shape), _const_spec((gw, d)),
                  _const_spec((1, d)), _const_spec((1, d))],
        out_specs=pl.BlockSpec((tm, d), lambda i: (i, 0)),
        out_shape=jax.ShapeDtypeStruct((t, d), F32),
        compiler_params=_params("parallel"),
        name="gmlp_ln",
    )(x, w_in, vn_g, vn_b, ws, bs, w_out, g, b)


def _rwkv_proj_kernel(*refs, seq_tiles, has_vres):
    if has_vres:
        (x_ref, xp_ref, mu_ref, wr_ref, wk_ref, wv_ref, w0_ref, w1_ref, w2_ref, a0_ref, a1_ref,
         a2_ref, g1_ref, g2_ref, vf_ref, v0_ref, v1_ref, v2_ref,
         r_out, lw_out, k_out, v_out, a_out, g_out) = refs
    else:
        (x_ref, xp_ref, mu_ref, wr_ref, wk_ref, wv_ref, w0_ref, w1_ref, w2_ref, a0_ref, a1_ref,
         a2_ref, g1_ref, g2_ref,
         r_out, lw_out, k_out, v_out, a_out, g_out) = refs
    x = x_ref[...]
    not_first = (pl.program_id(0) % seq_tiles != 0).astype(F32)
    prev = xp_ref[7:8, :] * not_first
    row = lax.broadcasted_iota(jnp.int32, x.shape, 0)
    shifted = jnp.where(row == 0, prev, pltpu.roll(x, shift=1, axis=0))
    xx = shifted - x
    mix = lambda i: (x + xx * mu_ref[i:i + 1, :]).astype(BF16)

    def emit(out, val):
        for p in range(out.shape[0]):
            out[p] = val[:, p * LANES:(p + 1) * LANES]

    emit(r_out, _bdot(mix(0), wr_ref[...]))
    w_in = w0_ref[...] + _bdot(jnp.tanh(_bdot(mix(1), w1_ref[...])), w2_ref[...])
    emit(k_out, _bdot(mix(2), wk_ref[...]))
    emit(lw_out, -math.exp(-0.5) * jax.nn.sigmoid(w_in))
    xv = mix(3)
    v = _bdot(xv, wv_ref[...])
    a_in = a0_ref[...] + _bdot(_bdot(mix(4), a1_ref[...]), a2_ref[...])
    if has_vres:
        gate_in = v0_ref[...] + _bdot(_bdot(xv, v1_ref[...]), v2_ref[...])
    g_hidden = _bdot(mix(5), g1_ref[...])
    emit(a_out, jax.nn.sigmoid(a_in))
    if has_vres:
        vf = jnp.concatenate([vf_ref[p] for p in range(vf_ref.shape[0])], axis=1)
        v = v + (vf - v) * jax.nn.sigmoid(gate_in)
    emit(v_out, v)
    emit(g_out, _bdot(jax.nn.sigmoid(g_hidden), g2_ref[...]))


def _rwkv_proj(h, seq, mu, wr, wk, wv, w0, w1, w2, a0, a1, a2, g1, g2, vres, tm=256):
    t, d = h.shape
    pairs = d // LANES
    has_vres = vres is not None
    row_spec = pl.BlockSpec((tm, d), lambda i: (i, 0))
    prev_spec = pl.BlockSpec((8, d), lambda i: (jnp.maximum(i * (tm // 8) - 1, 0), 0))
    pm_spec = pl.BlockSpec((pairs, tm, LANES), lambda i: (0, i, 0))
    args = [h, h, mu, wr, wk, wv, w0, w1, w2, a0, a1, a2, g1, g2]
    in_specs = [row_spec, prev_spec] + [_const_spec(a.shape) for a in args[2:]]
    if has_vres:
        v_first, v0, v1, v2 = vres
        args += [v_first, v0, v1, v2]
        in_specs += [pm_spec, _const_spec(v0.shape), _const_spec(v1.shape), _const_spec(v2.shape)]
    pm_shape = jax.ShapeDtypeStruct((pairs, t, LANES), F32)
    return pl.pallas_call(
        functools.partial(_rwkv_proj_kernel, seq_tiles=seq // tm, has_vres=has_vres),
        grid=(t // tm,),
        in_specs=in_specs,
        out_specs=[pm_spec] * 6,
        out_shape=[pm_shape] * 6,
        compiler_params=_params("parallel"),
        name="rwkv_proj",
    )(*args)


def _quarters(x, quarter_masks):
    zero = jnp.zeros_like(x)
    return jnp.concatenate([jnp.where(m, x, zero) for m in quarter_masks], axis=0)


def _head_transpose(x, lo_mask):
    halves = []
    for s in range(x.shape[1] // LANES):
        xs = x[:, s * LANES:(s + 1) * LANES]
        t = jnp.concatenate([xs, xs], axis=0).T
        halves.append(jnp.where(lo_mask, t[:RWKV_HEAD], t[RWKV_HEAD:]))
    return jnp.concatenate(halves, axis=1)


def _wkv_kernel(r_ref, lw_ref, k_ref, v_ref, a_ref, kk_ref, ka_ref, rk_ref, gg_ref, gb_ref,
                y_ref, h_ref, rt_s, nkt_s, bt_s, kk_s, vb_s, bonus_s, glast_s, *, seq_tiles):
    pairs, ts, _ = r_ref.shape
    half = pairs // 2
    rows = half * ts
    W = MXU_DIM
    L = WKV_CHUNK
    per_pair = ts // L
    n_chunks = half * per_pair
    step = pl.program_id(0)

    @pl.when(step == 0)
    def _():
        for ref in (h_ref, rt_s, nkt_s, bt_s, kk_s, vb_s, bonus_s, glast_s):
            ref[...] = jnp.zeros_like(ref)

    def widen(x):
        return jnp.concatenate([x[:half].reshape(rows, LANES), x[half:].reshape(rows, LANES)],
                               axis=1)

    per_row = lambda ref: widen(jnp.broadcast_to(ref[...], (pairs, ts, LANES)))

    li = lax.broadcasted_iota(jnp.int32, (L, W), 0)
    lj = lax.broadcasted_iota(jnp.int32, (L, W), 1)
    quarter_masks = [(lj // RWKV_HEAD) == q for q in range(W // RWKV_HEAD)]
    src = lj % RWKV_HEAD
    strict = li > src
    incl = li >= src
    eye = (li == src).astype(F32)
    lo_mask = lax.broadcasted_iota(jnp.int32, (L, LANES), 1) < RWKV_HEAD
    qi = lax.broadcasted_iota(jnp.int32, (W, W), 0)
    qj = lax.broadcasted_iota(jnp.int32, (W, W), 1)
    head_ones = ((qi // RWKV_HEAD) == (qj // RWKV_HEAD)).astype(BF16)
    ti = lax.broadcasted_iota(jnp.int32, (LANES, LANES), 0)
    tj = lax.broadcasted_iota(jnp.int32, (LANES, LANES), 1)
    tri = ((ti >= tj) & (ti // L == tj // L)).astype(BF16)

    chunks = range(n_chunks)
    bd = lambda x: _quarters(x.astype(BF16), quarter_masks)
    mm = lambda x, w: jnp.dot(x.astype(BF16), w, preferred_element_type=F32)
    bonus_prev = bonus_s[...]
    rt = [rt_s[c * L:(c + 1) * L] for c in chunks]
    nkt = [nkt_s[c * L:(c + 1) * L] for c in chunks]
    bt = [bt_s[c * L:(c + 1) * L] for c in chunks]
    kk_ = [kk_s[c * L:(c + 1) * L] for c in chunks]
    bd_v = [bd(vb_s[c * L:(c + 1) * L]) for c in chunks]
    g_last = [glast_s[c:c + 1, :] for c in chunks]

    att = [lax.dot_general(jnp.concatenate([nkt[c], rt[c].astype(BF16)], axis=0),
                           jnp.concatenate([bd(bt[c]), bd(kk_[c])], axis=0),
                           (((1,), (1,)), ((), ())), preferred_element_type=F32) for c in chunks]
    q = [jnp.where(strict, att[c][:L, :W], 0.0) for c in chunks]
    nak = [jnp.where(strict, att[c][:L, W:], 0.0).astype(BF16) for c in chunks]
    a_rb = [jnp.where(incl, att[c][L:, :W], 0.0).astype(BF16) for c in chunks]
    a_rk = [jnp.where(incl, att[c][L:, W:], 0.0).astype(BF16) for c in chunks]
    nav = [mm(nak[c], bd_v[c]) for c in chunks]
    be_t = [_head_transpose(bt[c] * g_last[c], lo_mask).astype(BF16) for c in chunks]
    ke_t = [_head_transpose(kk_[c] * g_last[c], lo_mask).astype(BF16) for c in chunks]
    g_key = []
    for c in chunks:
        sides = []
        for s in range(W // LANES):
            t = jnp.broadcast_to(g_last[c][:, s * LANES:(s + 1) * LANES], (LANES, LANES)).T
            sides.append(jnp.where(lo_mask, t[:RWKV_HEAD], t[RWKV_HEAD:]))
        g_key.append(jnp.concatenate(sides, axis=1))

    def prepare(i):
        slot = lambda ref: jnp.concatenate([ref[i], ref[i + half]], axis=1)
        sl = slice(i * ts, (i + 1) * ts)
        r, lw, k, v, a = slot(r_ref), slot(lw_ref), slot(k_ref), slot(v_ref), slot(a_ref)
        kraw = k * slot(kk_ref)
        ss = _bdot(kraw * kraw, head_ones)
        kappa = kraw / jnp.maximum(jnp.sqrt(ss), 1e-12)
        k2 = k * (1.0 + (a - 1.0) * slot(ka_ref))
        cs = jnp.concatenate([_dot_exact_lhs(tri, lw[j * LANES:(j + 1) * LANES])
                              for j in range(ts // LANES)], axis=0)
        g_incl = jnp.exp(cs)
        g_inv = jnp.exp(-cs)
        rt_s[sl] = r * g_incl
        nkt_s[sl] = (-kappa * jnp.exp(cs - lw)).astype(BF16)
        bt_s[sl] = kappa * a * g_inv
        kk_s[sl] = k2 * g_inv
        vb_s[sl] = v.astype(BF16)
        bonus_s[sl] = _bdot(r * k2 * slot(rk_ref), head_ones) * v
        glast_s[i * per_pair:(i + 1) * per_pair] = jnp.concatenate(
            [g_incl[(j + 1) * L - 1:(j + 1) * L, :] for j in range(per_pair)], axis=0)

    p = [eye + q[c] for c in chunks]
    qb = [q[c].astype(BF16) for c in chunks]
    q = [mm(qb[c], bd(qb[c])) for c in chunks]
    doublings = int(math.log2(L)) - 2
    assert half == doublings, "one prepare() slot per doubling step"
    for i in range(doublings):
        prepare(i)
        qb = [q[c].astype(BF16) for c in chunks]
        pq = [mm(jnp.concatenate([qb[c], p[c].astype(BF16)], axis=0), bd(qb[c])) for c in chunks]
        q = [pq[c][:L] for c in chunks]
        p = [p[c] + pq[c][L:] for c in chunks]
    p = [p[c] + mm(p[c], bd(q[c])) for c in chunks]
    tw = [mm(p[c], jnp.concatenate([bd(nkt[c]), bd(nav[c])], axis=1)) for c in chunks]
    nw = [bd(tw[c][:, :W]) for c in chunks]
    bd_uv = [jnp.concatenate([bd(tw[c][:, W:]), bd_v[c]], axis=0) for c in chunks]
    yg = [mm(jnp.concatenate([jnp.concatenate([a_rb[c], a_rk[c]], axis=1),
                              jnp.concatenate([be_t[c], ke_t[c]], axis=1)], axis=0), bd_uv[c])
          for c in chunks]
    rm = [mm(jnp.concatenate([a_rb[c], be_t[c]], axis=0), nw[c]) for c in chunks]
    lhs = [jnp.concatenate([rt[c] + rm[c][:L], rm[c][L:]], axis=0).astype(BF16) for c in chunks]

    seq_start = (step + seq_tiles - 1) % seq_tiles == 0
    hs = [jnp.where(seq_start, 0.0, h_ref[p_]) for p_ in range(half)]
    y_chunks = [None] * n_chunks
    for j in range(per_pair):
        for p_ in range(half):
            c = p_ * per_pair + j
            prod = mm(lhs[c], bd(hs[p_]))
            y_chunks[c] = yg[c][:L] + prod[:L]
            hs[p_] = g_key[c] * hs[p_] + prod[L:] + yg[c][L:]
    for p_ in range(half):
        h_ref[p_] = hs[p_]

    y = jnp.concatenate(y_chunks, axis=0)
    inv_n = 1.0 / RWKV_HEAD
    mean = _dot2_rhs(y, head_ones) * inv_n
    yc = y - mean
    var = _bdot(yc * yc, head_ones) * inv_n
    yn = yc * lax.rsqrt(var + GN_EPS) * per_row(gg_ref) + per_row(gb_ref)
    out = yn + bonus_prev
    y_ref[:half] = out[:, :LANES].reshape(half, ts, LANES)
    y_ref[half:] = out[:, LANES:].reshape(half, ts, LANES)


def _wkv(r, lw, k, v, a, k_k, k_a, r_k, gn_g, gn_b, seq, ts=128):
    pairs, t, _ = r.shape
    tiles = t // ts
    half = pairs // 2
    rows = half * ts
    in_spec = pl.BlockSpec((pairs, ts, LANES), lambda n: (0, jnp.minimum(n, tiles - 1), 0))
    out_spec = pl.BlockSpec((pairs, ts, LANES), lambda n: (0, jnp.maximum(n - 1, 0), 0))
    stage_f32 = pltpu.VMEM((rows, MXU_DIM), F32)
    stage_bf16 = pltpu.VMEM((rows, MXU_DIM), BF16)
    return pl.pallas_call(
        functools.partial(_wkv_kernel, seq_tiles=seq // ts),
        grid=(tiles + 1,),
        in_specs=[in_spec] * 5 + [_const_spec((pairs, 1, LANES))] * 5,
        out_specs=out_spec,
        out_shape=jax.ShapeDtypeStruct((pairs, t, LANES), F32),
        scratch_shapes=[pltpu.VMEM((half, RWKV_HEAD, MXU_DIM), F32),
                        stage_f32, stage_bf16, stage_f32, stage_f32, stage_bf16, stage_f32,
                        pltpu.VMEM((rows // WKV_CHUNK, MXU_DIM), F32)],
        compiler_params=_params("arbitrary"),
        name="wkv7_chunked",
    )(r, lw, k, v, a, k_k, k_a, r_k, gn_g, gn_b)


def _rwkv_out_kernel(y_ref, g_ref, h_ref, wo_ref, lg_ref, lb_ref, o_ref, *, alpha, parts):
    pairs, tm, _ = y_ref.shape
    pm = tm // parts
    mixes = []
    for i in range(parts):
        rows = slice(i * pm, (i + 1) * pm)
        yg = jnp.concatenate([(y_ref[p, rows] * g_ref[p, rows]).astype(BF16)
                              for p in range(pairs)], axis=1)
        mixes.append(jnp.dot(yg, wo_ref[...], preferred_element_type=F32))
    for i in range(parts):
        rows = slice(i * pm, (i + 1) * pm)
        o_ref[rows] = _layer_norm(alpha * h_ref[rows] + mixes[i], lg_ref[...], lb_ref[...], LN_EPS)


def _rwkv_out(y, g, h, w_o, ln_g, ln_b, alpha, tm=1024, parts=4):
    t, d = h.shape
    pairs = d // LANES
    pm_spec = pl.BlockSpec((pairs, tm, LANES), lambda i: (0, i, 0))
    row_spec = pl.BlockSpec((tm, d), lambda i: (i, 0))
    return pl.pallas_call(
        functools.partial(_rwkv_out_kernel, alpha=alpha, parts=parts),
        grid=(t // tm,),
        in_specs=[pm_spec, pm_spec, row_spec, _const_spec((d, d)), _const_spec((1, d)),
                  _const_spec((1, d))],
        out_specs=row_spec,
        out_shape=jax.ShapeDtypeStruct((t, d), F32),
        compiler_params=_params("parallel"),
        name="rwkv_out_ln",
    )(y, g, h, w_o, ln_g, ln_b)


def _pair_major(vec):
    return vec.reshape(-1, 1, LANES)


def kernel(x, ln1_g, ln1_b, ln2_g, ln2_b, ffn_w1, ffn_w2, gm_w_in, gm_vn_g, gm_vn_b, gm_ws, gm_bs, gm_w_out, rw_mu, rw_w_rkv, rw_w0, rw_w1, rw_w2, rw_a0, rw_a1, rw_a2, rw_g1, rw_g2, rw_k_k, rw_k_a, rw_r_k, rw_gn_g, rw_gn_b, rw_w_o, rw_v0, rw_v1, rw_v2):
    batch, seq, d = x.shape
    depth = ln1_g.shape[0]
    alpha = (2.0 * depth) ** 0.25
    h = x.reshape(batch * seq, d)
    row = lambda vec: vec.reshape(1, -1)
    causal = jnp.tril(jnp.ones((GMLP_CHUNK, GMLP_CHUNK), dtype=bool))
    v_first = None
    for i in range(depth):
        j = i // 2
        if i % 2 == 0:
            ws = jnp.where(causal[None], gm_ws[j], 0.0).astype(BF16)
            bs = jnp.broadcast_to(gm_bs[j][:, :, None], gm_bs[j].shape + (d // GMLP_GROUPS,))
            h = _gmlp(h, gm_w_in[j].astype(BF16), row(gm_vn_g[j]), row(gm_vn_b[j]), ws, bs,
                      gm_w_out[j].astype(BF16), row(ln1_g[i]), row(ln1_b[i]), alpha)
        else:
            vres = None
            if j > 0:
                vres = (v_first, row(rw_v0[j - 1]), rw_v1[j - 1].astype(BF16),
                        rw_v2[j - 1].astype(BF16))
            r, lw, k, v, a, g = _rwkv_proj(
                h, seq, rw_mu[j], rw_w_rkv[j, 0].astype(BF16), rw_w_rkv[j, 1].astype(BF16),
                rw_w_rkv[j, 2].astype(BF16), row(rw_w0[j]), rw_w1[j].astype(BF16),
                rw_w2[j].astype(BF16), row(rw_a0[j]), rw_a1[j].astype(BF16),
                rw_a2[j].astype(BF16), rw_g1[j].astype(BF16), rw_g2[j].astype(BF16), vres)
            if v_first is None:
                v_first = v
            y = _wkv(r, lw, k, v, a, _pair_major(rw_k_k[j]), _pair_major(rw_k_a[j]),
                     _pair_major(rw_r_k[j].reshape(-1)), _pair_major(rw_gn_g[j]),
                     _pair_major(rw_gn_b[j]), seq)
            h = _rwkv_out(y, g, h, rw_w_o[j].astype(BF16), row(ln1_g[i]), row(ln1_b[i]), alpha)
        h = _ffn(h, ffn_w1[i].astype(BF16), ffn_w2[i].astype(BF16), row(ln2_g[i]), row(ln2_b[i]),
                 alpha)
    return h.reshape(batch, seq, d)
```

```python
import functools
import math

import jax
import jax.numpy as jnp
from jax import lax
from jax.experimental import pallas as pl
from jax.experimental.pallas import tpu as pltpu

F32 = jnp.float32
BF16 = jnp.bfloat16

GMLP_CHUNK = 128
GMLP_GROUPS = 8
RWKV_HEAD = 64
LN_EPS = 1e-5
GN_EPS = 64e-5

LANES = 128
MXU_DIM = 256
WKV_CHUNK = 64
VMEM_LIMIT = 56 * 1024 * 1024


def _bdot(a, b):
    return jnp.dot(a.astype(BF16), b.astype(BF16), preferred_element_type=F32)


def _split3(x):
    hi = x.astype(BF16)
    r1 = x - hi.astype(F32)
    mid = r1.astype(BF16)
    lo = (r1 - mid.astype(F32)).astype(BF16)
    return hi, mid, lo


def _dot_exact_lhs(a_bf16, x):
    hi, mid, lo = _split3(x)
    out = jnp.dot(a_bf16, hi, preferred_element_type=F32)
    out += jnp.dot(a_bf16, mid, preferred_element_type=F32)
    out += jnp.dot(a_bf16, lo, preferred_element_type=F32)
    return out


def _dot2_rhs(x, b_bf16):
    hi = x.astype(BF16)
    lo = (x - hi.astype(F32)).astype(BF16)
    return (jnp.dot(hi, b_bf16, preferred_element_type=F32)
            + jnp.dot(lo, b_bf16, preferred_element_type=F32))


def _layer_norm(x, g, b, eps):
    mean = jnp.mean(x, axis=-1, keepdims=True)
    xc = x - mean
    var = jnp.mean(xc * xc, axis=-1, keepdims=True)
    return xc * lax.rsqrt(var + eps) * g + b


def _const_spec(shape):
    nd = len(shape)
    return pl.BlockSpec(shape, lambda *_: (0,) * nd, pipeline_mode=pl.Buffered(1))


def _params(*sem):
    return pltpu.CompilerParams(dimension_semantics=sem, vmem_limit_bytes=VMEM_LIMIT)


def _ffn_body(h, w1_ref, w2_ref, g_ref, b_ref, alpha, ff_block):
    hb = h.astype(BF16)
    d_ff = w1_ref.shape[1]
    acc = alpha * h
    for j in range(d_ff // ff_block):
        a = jnp.dot(hb, w1_ref[:, j * ff_block:(j + 1) * ff_block], preferred_element_type=F32)
        a = jnp.maximum(a, 0.0)
        a = (a * a).astype(BF16)
        acc += jnp.dot(a, w2_ref[j * ff_block:(j + 1) * ff_block, :], preferred_element_type=F32)
    return _layer_norm(acc, g_ref[...], b_ref[...], LN_EPS)


def _ffn_kernel(h_ref, w1_ref, w2_ref, g_ref, b_ref, o_ref, *, alpha, ff_block):
    o_ref[...] = _ffn_body(h_ref[...], w1_ref, w2_ref, g_ref, b_ref, alpha, ff_block)


def _out_ffn_kernel(y_ref, g_ref, h_ref, wo_ref, l1g_ref, l1b_ref, w1_ref, w2_ref, l2g_ref, l2b_ref,
                    o_ref, *, alpha, ff_block, parts):
    pm = h_ref.shape[0] // parts
    rows = [slice(i * pm, (i + 1) * pm) for i in range(parts)]
    mixes = [jnp.dot((y_ref[r] * g_ref[r]).astype(BF16), wo_ref[...], preferred_element_type=F32)
             for r in rows]
    h1 = jnp.concatenate([_layer_norm(alpha * h_ref[r] + m, l1g_ref[...], l1b_ref[...], LN_EPS)
                          for r, m in zip(rows, mixes)], axis=0)
    o_ref[...] = _ffn_body(h1, w1_ref, w2_ref, l2g_ref, l2b_ref, alpha, ff_block)


def _out_ffn(y, g, h, w_o, ln1_g, ln1_b, w1, w2, ln2_g, ln2_b, alpha, tm=512, ff_block=1024,
             parts=2):
    t, d = h.shape
    d_ff = w1.shape[1]
    row_spec = pl.BlockSpec((tm, d), lambda i: (i, 0))
    return pl.pallas_call(
        functools.partial(_out_ffn_kernel, alpha=alpha, ff_block=ff_block, parts=parts),
        grid=(t // tm,),
        in_specs=[row_spec, row_spec, row_spec, _const_spec((d, d)), _const_spec((1, d)),
                  _const_spec((1, d)), _const_spec((d, d_ff)), _const_spec((d_ff, d)),
                  _const_spec((1, d)), _const_spec((1, d))],
        out_specs=row_spec,
        out_shape=jax.ShapeDtypeStruct((t, d), F32),
        compiler_params=_params("parallel"),
        name="rwkv_out_ffn_ln",
    )(y, g, h, w_o, ln1_g, ln1_b, w1, w2, ln2_g, ln2_b)


def _ffn(h, w1, w2, g, b, alpha, tm=512, ff_block=1024):
    t, d = h.shape
    d_ff = w1.shape[1]
    return pl.pallas_call(
        functools.partial(_ffn_kernel, alpha=alpha, ff_block=ff_block),
        grid=(t // tm,),
        in_specs=[pl.BlockSpec((tm, d), lambda i: (i, 0)),
                  _const_spec((d, d_ff)), _const_spec((d_ff, d)),
                  _const_spec((1, d)), _const_spec((1, d))],
        out_specs=pl.BlockSpec((tm, d), lambda i: (i, 0)),
        out_shape=jax.ShapeDtypeStruct((t, d), F32),
        compiler_params=_params("parallel"),
        name="ffn_ln",
    )(h, w1, w2, g, b)


def _gmlp_kernel(x_ref, win_ref, vng_ref, vnb_ref, ws_ref, bs_ref, wout_ref, g_ref, b_ref, o_ref,
                 *, alpha, parts):
    tm = x_ref.shape[0]
    gw = wout_ref.shape[0]
    gd = gw // GMLP_GROUPS
    pm = tm // parts
    xs = [x_ref[i * pm:(i + 1) * pm] for i in range(parts)]
    zs = [jnp.dot(xs[i].astype(BF16), win_ref[...], preferred_element_type=F32)
          for i in range(parts)]
    for i in range(parts):
        z = zs[i]
        z = 0.5 * z * (1.0 + lax.erf(z * math.sqrt(0.5)))
        u = z[:, :gw]
        v = _layer_norm(z[:, gw:], vng_ref[...], vnb_ref[...], LN_EPS).astype(BF16)
        rows = []
        for c in range(pm // GMLP_CHUNK):
            cols = []
            for g in range(GMLP_GROUPS):
                vb = v[c * GMLP_CHUNK:(c + 1) * GMLP_CHUNK, g * gd:(g + 1) * gd]
                cols.append(jnp.dot(ws_ref[g], vb, preferred_element_type=F32) + bs_ref[g])
            rows.append(jnp.concatenate(cols, axis=1))
        mixed = jnp.concatenate(rows, axis=0)
        mix = jnp.dot((u * mixed).astype(BF16), wout_ref[...], preferred_element_type=F32)
        o_ref[i * pm:(i + 1) * pm] = _layer_norm(alpha * xs[i] + mix, g_ref[...], b_ref[...],
                                                  LN_EPS)


def _gmlp(x, w_in, vn_g, vn_b, ws, bs, w_out, g, b, alpha, tm=1024, parts=4):
    t, d = x.shape
    gw = w_out.shape[0]
    return pl.pallas_call(
        functools.partial(_gmlp_kernel, alpha=alpha, parts=parts),
        grid=(t // tm,),
        in_specs=[pl.BlockSpec((tm, d), lambda i: (i, 0)),
                  _const_spec((d, 2 * gw)), _const_spec((1, gw)), _const_spec((1, gw)),
                  _const_spec(ws.shape), _const_spec(bs.shape), _const_spec((gw, d)),
                  _const_spec((1, d)), _const_spec((1, d))],
        out_specs=pl.BlockSpec((tm, d), lambda i: (i, 0)),
        out_shape=jax.ShapeDtypeStruct((t, d), F32),
        compiler_params=_params("parallel"),
        name="gmlp_ln",
    )(x, w_in, vn_g, vn_b, ws, bs, w_out, g, b)


def _rwkv_proj_kernel(*refs, seq_tiles, has_vres):
    if has_vres:
        (x_ref, xp_ref, mu_ref, wr_ref, wk_ref, wv_ref, w0_ref, w1_ref, w2_ref, a0_ref, a1_ref,
         a2_ref, g1_ref, g2_ref, vf_ref, v0_ref, v1_ref, v2_ref,
         r_out, lw_out, k_out, v_out, a_out, g_out) = refs
    else:
        (x_ref, xp_ref, mu_ref, wr_ref, wk_ref, wv_ref, w0_ref, w1_ref, w2_ref, a0_ref, a1_ref,
         a2_ref, g1_ref, g2_ref,
         r_out, lw_out, k_out, v_out, a_out, g_out) = refs
    x = x_ref[...]
    not_first = (pl.program_id(0) % seq_tiles != 0).astype(F32)
    prev = xp_ref[7:8, :] * not_first
    row = lax.broadcasted_iota(jnp.int32, x.shape, 0)
    shifted = jnp.where(row == 0, prev, pltpu.roll(x, shift=1, axis=0))
    xx = shifted - x
    mix = lambda i: (x + xx * mu_ref[i:i + 1, :]).astype(BF16)

    r_out[...] = _bdot(mix(0), wr_ref[...])
    w_in = w0_ref[...] + _bdot(jnp.tanh(_bdot(mix(1), w1_ref[...])), w2_ref[...])
    k_out[...] = _bdot(mix(2), wk_ref[...])
    lw_out[...] = -math.exp(-0.5) * jax.nn.sigmoid(w_in)
    xv = mix(3)
    v = _bdot(xv, wv_ref[...])
    a_in = a0_ref[...] + _bdot(_bdot(mix(4), a1_ref[...]), a2_ref[...])
    if has_vres:
        gate_in = v0_ref[...] + _bdot(_bdot(xv, v1_ref[...]), v2_ref[...])
    g_hidden = _bdot(mix(5), g1_ref[...])
    a_out[...] = jax.nn.sigmoid(a_in)
    if has_vres:
        v = v + (vf_ref[...] - v) * jax.nn.sigmoid(gate_in)
    v_out[...] = v
    g_out[...] = _bdot(jax.nn.sigmoid(g_hidden), g2_ref[...])


def _rwkv_proj(h, seq, weights, vres, tm=256):
    t, d = h.shape
    has_vres = vres is not None
    row_spec = pl.BlockSpec((tm, d), lambda i: (i, 0))
    prev_spec = pl.BlockSpec((8, d), lambda i: (jnp.maximum(i * (tm // 8) - 1, 0), 0))
    args = [h, h] + list(weights)
    in_specs = [row_spec, prev_spec] + [_const_spec(w.shape) for w in weights]
    if has_vres:
        v_first, v0, v1, v2 = vres
        args += [v_first, v0, v1, v2]
        in_specs += [row_spec, _const_spec(v0.shape), _const_spec(v1.shape), _const_spec(v2.shape)]
    return pl.pallas_call(
        functools.partial(_rwkv_proj_kernel, seq_tiles=seq // tm, has_vres=has_vres),
        grid=(t // tm,),
        in_specs=in_specs,
        out_specs=[row_spec] * 6,
        out_shape=[jax.ShapeDtypeStruct((t, d), F32)] * 6,
        compiler_params=_params("parallel"),
        name="rwkv_proj",
    )(*args)


def _quarters(x, quarter_masks):
    zero = jnp.zeros_like(x)
    return jnp.concatenate([jnp.where(m, x, zero) for m in quarter_masks], axis=0)


def _head_transpose(x, lo_mask):
    halves = []
    for s in range(x.shape[1] // LANES):
        xs = x[:, s * LANES:(s + 1) * LANES]
        t = jnp.concatenate([xs, xs], axis=0).T
        halves.append(jnp.where(lo_mask, t[:RWKV_HEAD], t[RWKV_HEAD:]))
    return jnp.concatenate(halves, axis=1)


def _wkv_kernel(r_ref, lw_ref, k_ref, v_ref, a_ref, kk_ref, ka_ref, rk_ref, gg_ref, gb_ref,
                y_ref, h_ref, rt_s, nkt_s, bt_s, kk_s, vb_s, bonus_s, glast_s, *, seq_tiles):
    ts, d = r_ref.shape
    pairs = d // LANES
    half = pairs // 2
    rows = half * ts
    W = MXU_DIM
    L = WKV_CHUNK
    per_pair = ts // L
    n_chunks = half * per_pair
    step = pl.program_id(0)

    @pl.when(step == 0)
    def _():
        for ref in (h_ref, rt_s, nkt_s, bt_s, kk_s, vb_s, bonus_s, glast_s):
            ref[...] = jnp.zeros_like(ref)

    def slot_cols(x, i):
        return jnp.concatenate([x[:, i * LANES:(i + 1) * LANES],
                                x[:, (i + half) * LANES:(i + half + 1) * LANES]], axis=1)

    def per_row(ref):
        p = ref[...]
        return jnp.concatenate([jnp.broadcast_to(slot_cols(p, i), (ts, W)) for i in range(half)],
                               axis=0)

    li = lax.broadcasted_iota(jnp.int32, (L, W), 0)
    lj = lax.broadcasted_iota(jnp.int32, (L, W), 1)
    quarter_masks = [(lj // RWKV_HEAD) == q for q in range(W // RWKV_HEAD)]
    src = lj % RWKV_HEAD
    strict = li > src
    incl = li >= src
    eye = (li == src).astype(F32)
    lo_mask = lax.broadcasted_iota(jnp.int32, (L, LANES), 1) < RWKV_HEAD
    qi = lax.broadcasted_iota(jnp.int32, (W, W), 0)
    qj = lax.broadcasted_iota(jnp.int32, (W, W), 1)
    head_ones = ((qi // RWKV_HEAD) == (qj // RWKV_HEAD)).astype(BF16)
    ti = lax.broadcasted_iota(jnp.int32, (LANES, LANES), 0)
    tj = lax.broadcasted_iota(jnp.int32, (LANES, LANES), 1)
    tri = ((ti >= tj) & (ti // L == tj // L)).astype(BF16)

    chunks = range(n_chunks)
    bd = lambda x: _quarters(x.astype(BF16), quarter_masks)
    mm = lambda x, w: jnp.dot(x.astype(BF16), w, preferred_element_type=F32)
    bonus_prev = bonus_s[...]
    rt = [rt_s[c * L:(c + 1) * L] for c in chunks]
    nkt = [nkt_s[c * L:(c + 1) * L] for c in chunks]
    bt = [bt_s[c * L:(c + 1) * L] for c in chunks]
    kk_ = [kk_s[c * L:(c + 1) * L] for c in chunks]
    bd_v = [bd(vb_s[c * L:(c + 1) * L]) for c in chunks]
    g_last = [glast_s[c:c + 1, :] for c in chunks]

    att = [lax.dot_general(jnp.concatenate([nkt[c], rt[c].astype(BF16)], axis=0),
                           jnp.concatenate([bd(bt[c]), bd(kk_[c])], axis=0),
                           (((1,), (1,)), ((), ())), preferred_element_type=F32) for c in chunks]
    q = [jnp.where(strict, att[c][:L, :W], 0.0) for c in chunks]
    nak = [jnp.where(strict, att[c][:L, W:], 0.0).astype(BF16) for c in chunks]
    a_rb = [jnp.where(incl, att[c][L:, :W], 0.0).astype(BF16) for c in chunks]
    a_rk = [jnp.where(incl, att[c][L:, W:], 0.0).astype(BF16) for c in chunks]
    nav = [mm(nak[c], bd_v[c]) for c in chunks]
    be_t = [_head_transpose(bt[c] * g_last[c], lo_mask).astype(BF16) for c in chunks]
    ke_t = [_head_transpose(kk_[c] * g_last[c], lo_mask).astype(BF16) for c in chunks]
    g_key = []
    for c in chunks:
        sides = []
        for s in range(W // LANES):
            t = jnp.broadcast_to(g_last[c][:, s * LANES:(s + 1) * LANES], (LANES, LANES)).T
            sides.append(jnp.where(lo_mask, t[:RWKV_HEAD], t[RWKV_HEAD:]))
        g_key.append(jnp.concatenate(sides, axis=1))

    def prepare(i):
        slot = lambda ref: jnp.concatenate(
            [ref[:, i * LANES:(i + 1) * LANES],
             ref[:, (i + half) * LANES:(i + half + 1) * LANES]], axis=1)
        sl = slice(i * ts, (i + 1) * ts)
        r, lw, k, v, a = slot(r_ref), slot(lw_ref), slot(k_ref), slot(v_ref), slot(a_ref)
        kraw = k * slot(kk_ref)
        ss = _bdot(kraw * kraw, head_ones)
        kappa = kraw / jnp.maximum(jnp.sqrt(ss), 1e-12)
        k2 = k * (1.0 + (a - 1.0) * slot(ka_ref))
        cs = jnp.concatenate([_dot_exact_lhs(tri, lw[j * LANES:(j + 1) * LANES])
                              for j in range(ts // LANES)], axis=0)
        g_incl = jnp.exp(cs)
        g_inv = jnp.exp(-cs)
        rt_s[sl] = r * g_incl
        nkt_s[sl] = (-kappa * jnp.exp(cs - lw)).astype(BF16)
        bt_s[sl] = kappa * a * g_inv
        kk_s[sl] = k2 * g_inv
        vb_s[sl] = v.astype(BF16)
        bonus_s[sl] = _bdot(r * k2 * slot(rk_ref), head_ones) * v
        glast_s[i * per_pair:(i + 1) * per_pair] = jnp.concatenate(
            [g_incl[(j + 1) * L - 1:(j + 1) * L, :] for j in range(per_pair)], axis=0)

    p = [eye + q[c] for c in chunks]
    qb = [q[c].astype(BF16) for c in chunks]
    q = [mm(qb[c], bd(qb[c])) for c in chunks]
    doublings = int(math.log2(L)) - 2
    assert half == doublings, "one prepare() slot per doubling step"
    for i in range(doublings):
        prepare(i)
        qb = [q[c].astype(BF16) for c in chunks]
        pq = [mm(jnp.concatenate([qb[c], p[c].astype(BF16)], axis=0), bd(qb[c])) for c in chunks]
        q = [pq[c][:L] for c in chunks]
        p = [p[c] + pq[c][L:] for c in chunks]
    p = [p[c] + mm(p[c], bd(q[c])) for c in chunks]
    tw = [mm(p[c], jnp.concatenate([bd(nkt[c]), bd(nav[c])], axis=1)) for c in chunks]
    nw = [bd(tw[c][:, :W]) for c in chunks]
    bd_uv = [jnp.concatenate([bd(tw[c][:, W:]), bd_v[c]], axis=0) for c in chunks]
    yg = [mm(jnp.concatenate([jnp.concatenate([a_rb[c], a_rk[c]], axis=1),
                              jnp.concatenate([be_t[c], ke_t[c]], axis=1)], axis=0), bd_uv[c])
          for c in chunks]
    rm = [mm(jnp.concatenate([a_rb[c], be_t[c]], axis=0), nw[c]) for c in chunks]
    lhs = [jnp.concatenate([rt[c] + rm[c][:L], rm[c][L:]], axis=0).astype(BF16) for c in chunks]

    seq_start = (step + seq_tiles - 1) % seq_tiles == 0
    hs = [jnp.where(seq_start, 0.0, h_ref[p_]) for p_ in range(half)]
    y_chunks = [None] * n_chunks
    for j in range(per_pair):
        for p_ in range(half):
            c = p_ * per_pair + j
            prod = mm(lhs[c], bd(hs[p_]))
            y_chunks[c] = yg[c][:L] + prod[:L]
            hs[p_] = g_key[c] * hs[p_] + prod[L:] + yg[c][L:]
    for p_ in range(half):
        h_ref[p_] = hs[p_]

    y = jnp.concatenate(y_chunks, axis=0)
    inv_n = 1.0 / RWKV_HEAD
    mean = _dot2_rhs(y, head_ones) * inv_n
    yc = y - mean
    var = _bdot(yc * yc, head_ones) * inv_n
    yn = yc * lax.rsqrt(var + GN_EPS) * per_row(gg_ref) + per_row(gb_ref)
    out = yn + bonus_prev
    for i in range(half):
        y_ref[:, i * LANES:(i + 1) * LANES] = out[i * ts:(i + 1) * ts, :LANES]
        y_ref[:, (i + half) * LANES:(i + half + 1) * LANES] = out[i * ts:(i + 1) * ts, LANES:]


def _wkv(r, lw, k, v, a, head_params, seq, ts=128):
    t, d = r.shape
    tiles = t // ts
    rows = (d // MXU_DIM) * ts
    in_spec = pl.BlockSpec((ts, d), lambda n: (jnp.minimum(n, tiles - 1), 0))
    out_spec = pl.BlockSpec((ts, d), lambda n: (jnp.maximum(n - 1, 0), 0))
    stage_f32 = pltpu.VMEM((rows, MXU_DIM), F32)
    stage_bf16 = pltpu.VMEM((rows, MXU_DIM), BF16)
    return pl.pallas_call(
        functools.partial(_wkv_kernel, seq_tiles=seq // ts),
        grid=(tiles + 1,),
        in_specs=[in_spec] * 5 + [_const_spec((1, d))] * len(head_params),
        out_specs=out_spec,
        out_shape=jax.ShapeDtypeStruct((t, d), F32),
        scratch_shapes=[pltpu.VMEM((d // MXU_DIM, RWKV_HEAD, MXU_DIM), F32),
                        stage_f32, stage_bf16, stage_f32, stage_f32, stage_bf16, stage_f32,
                        pltpu.VMEM((rows // WKV_CHUNK, MXU_DIM), F32)],
        compiler_params=_params("arbitrary"),
        name="wkv7_chunked",
    )(r, lw, k, v, a, *head_params)


def kernel(x, ln1_g, ln1_b, ln2_g, ln2_b, ffn_w1, ffn_w2, gm_w_in, gm_vn_g, gm_vn_b, gm_ws, gm_bs, gm_w_out, rw_mu, rw_w_rkv, rw_w0, rw_w1, rw_w2, rw_a0, rw_a1, rw_a2, rw_g1, rw_g2, rw_k_k, rw_k_a, rw_r_k, rw_gn_g, rw_gn_b, rw_w_o, rw_v0, rw_v1, rw_v2):
    batch, seq, d = x.shape
    depth = ln1_g.shape[0]
    alpha = (2.0 * depth) ** 0.25
    h = x.reshape(batch * seq, d)
    row = lambda vec: vec.reshape(1, -1)
    causal = jnp.tril(jnp.ones((GMLP_CHUNK, GMLP_CHUNK), dtype=bool))
    v_first = None
    for i in range(depth):
        j = i // 2
        if i % 2 == 0:
            ws = jnp.where(causal[None], gm_ws[j], 0.0).astype(BF16)
            bs = jnp.broadcast_to(gm_bs[j][:, :, None], gm_bs[j].shape + (d // GMLP_GROUPS,))
            h = _gmlp(h, gm_w_in[j].astype(BF16), row(gm_vn_g[j]), row(gm_vn_b[j]), ws, bs,
                      gm_w_out[j].astype(BF16), row(ln1_g[i]), row(ln1_b[i]), alpha)
            h = _ffn(h, ffn_w1[i].astype(BF16), ffn_w2[i].astype(BF16), row(ln2_g[i]),
                     row(ln2_b[i]), alpha)
        else:
            vres = None
            if j > 0:
                vres = (v_first, row(rw_v0[j - 1]), rw_v1[j - 1].astype(BF16),
                        rw_v2[j - 1].astype(BF16))
            proj_weights = (rw_mu[j], rw_w_rkv[j, 0].astype(BF16), rw_w_rkv[j, 1].astype(BF16),
                            rw_w_rkv[j, 2].astype(BF16), row(rw_w0[j]), rw_w1[j].astype(BF16),
                            rw_w2[j].astype(BF16), row(rw_a0[j]), rw_a1[j].astype(BF16),
                            rw_a2[j].astype(BF16), rw_g1[j].astype(BF16), rw_g2[j].astype(BF16))
            head_params = (row(rw_k_k[j]), row(rw_k_a[j]), row(rw_r_k[j]), row(rw_gn_g[j]),
                           row(rw_gn_b[j]))
            r, lw, k, v, a, g = _rwkv_proj(h, seq, proj_weights, vres)
            if v_first is None:
                v_first = v
            y = _wkv(r, lw, k, v, a, head_params, seq)
            h = _out_ffn(y, g, h, rw_w_o[j].astype(BF16), row(ln1_g[i]), row(ln1_b[i]),
                         ffn_w1[i].astype(BF16), ffn_w2[i].astype(BF16), row(ln2_g[i]),
                         row(ln2_b[i]), alpha)
    return h.reshape(batch, seq, d)
```

```python
import functools
import math

import jax
import jax.numpy as jnp
from jax import lax
from jax.experimental import pallas as pl
from jax.experimental.pallas import tpu as pltpu

F32 = jnp.float32
BF16 = jnp.bfloat16

GMLP_CHUNK = 128
GMLP_GROUPS = 8
RWKV_HEAD = 64
LN_EPS = 1e-5
GN_EPS = 64e-5

LANES = 128
MXU_DIM = 256
WKV_CHUNK = 64
VMEM_LIMIT = 56 * 1024 * 1024


def _bdot(a, b):
    return jnp.dot(a.astype(BF16), b.astype(BF16), preferred_element_type=F32)


def _split3(x):
    hi = x.astype(BF16)
    r1 = x - hi.astype(F32)
    mid = r1.astype(BF16)
    lo = (r1 - mid.astype(F32)).astype(BF16)
    return hi, mid, lo


def _dot_exact_lhs(a_bf16, x):
    hi, mid, lo = _split3(x)
    out = jnp.dot(a_bf16, hi, preferred_element_type=F32)
    out += jnp.dot(a_bf16, mid, preferred_element_type=F32)
    out += jnp.dot(a_bf16, lo, preferred_element_type=F32)
    return out


def _dot2_rhs(x, b_bf16):
    hi = x.astype(BF16)
    lo = (x - hi.astype(F32)).astype(BF16)
    n = x.shape[0]
    both = jnp.dot(jnp.concatenate([hi, lo], axis=0), b_bf16, preferred_element_type=F32)
    return both[:n] + both[n:]


def _layer_norm(x, g, b, eps):
    mean = jnp.mean(x, axis=-1, keepdims=True)
    xc = x - mean
    var = jnp.mean(xc * xc, axis=-1, keepdims=True)
    return xc * lax.rsqrt(var + eps) * g + b


def _const_spec(shape):
    nd = len(shape)
    return pl.BlockSpec(shape, lambda *_: (0,) * nd, pipeline_mode=pl.Buffered(1))


def _params(*sem):
    return pltpu.CompilerParams(dimension_semantics=sem, vmem_limit_bytes=VMEM_LIMIT)


def _ffn_body(h, w1_ref, w2_ref, g_ref, b_ref, alpha, ff_block):
    hb = h.astype(BF16)
    d_ff = w1_ref.shape[1]
    acc = alpha * h
    for j in range(d_ff // ff_block):
        a = jnp.dot(hb, w1_ref[:, j * ff_block:(j + 1) * ff_block], preferred_element_type=F32)
        a = jnp.maximum(a, 0.0)
        a = (a * a).astype(BF16)
        acc += jnp.dot(a, w2_ref[j * ff_block:(j + 1) * ff_block, :], preferred_element_type=F32)
    return _layer_norm(acc, g_ref[...], b_ref[...], LN_EPS)


def _ffn_kernel(h_ref, w1_ref, w2_ref, g_ref, b_ref, o_ref, *, alpha, ff_block):
    o_ref[...] = _ffn_body(h_ref[...], w1_ref, w2_ref, g_ref, b_ref, alpha, ff_block)


def _out_ffn_kernel(y_ref, g_ref, h_ref, wo_ref, l1g_ref, l1b_ref, w1_ref, w2_ref, l2g_ref, l2b_ref,
                    o_ref, *, alpha, ff_block, parts):
    pm = h_ref.shape[0] // parts
    rows = [slice(i * pm, (i + 1) * pm) for i in range(parts)]
    mixes = [jnp.dot((y_ref[r] * g_ref[r]).astype(BF16), wo_ref[...], preferred_element_type=F32)
             for r in rows]
    h1 = jnp.concatenate([_layer_norm(alpha * h_ref[r] + m, l1g_ref[...], l1b_ref[...], LN_EPS)
                          for r, m in zip(rows, mixes)], axis=0)
    o_ref[...] = _ffn_body(h1, w1_ref, w2_ref, l2g_ref, l2b_ref, alpha, ff_block)


def _out_ffn(y, g, h, w_o, ln1_g, ln1_b, w1, w2, ln2_g, ln2_b, alpha, tm=512, ff_block=1024,
             parts=2):
    t, d = h.shape
    d_ff = w1.shape[1]
    row_spec = pl.BlockSpec((tm, d), lambda i: (i, 0))
    return pl.pallas_call(
        functools.partial(_out_ffn_kernel, alpha=alpha, ff_block=ff_block, parts=parts),
        grid=(t // tm,),
        in_specs=[row_spec, row_spec, row_spec, _const_spec((d, d)), _const_spec((1, d)),
                  _const_spec((1, d)), _const_spec((d, d_ff)), _const_spec((d_ff, d)),
                  _const_spec((1, d)), _const_spec((1, d))],
        out_specs=row_spec,
        out_shape=jax.ShapeDtypeStruct((t, d), F32),
        compiler_params=_params("parallel"),
        name="rwkv_out_ffn_ln",
    )(y, g, h, w_o, ln1_g, ln1_b, w1, w2, ln2_g, ln2_b)


def _ffn(h, w1, w2, g, b, alpha, tm=512, ff_block=1024):
    t, d = h.shape
    d_ff = w1.shape[1]
    return pl.pallas_call(
        functools.partial(_ffn_kernel, alpha=alpha, ff_block=ff_block),
        grid=(t // tm,),
        in_specs=[pl.BlockSpec((tm, d), lambda i: (i, 0)),
                  _const_spec((d, d_ff)), _const_spec((d_ff, d)),
                  _const_spec((1, d)), _const_spec((1, d))],
        out_specs=pl.BlockSpec((tm, d), lambda i: (i, 0)),
        out_shape=jax.ShapeDtypeStruct((t, d), F32),
        compiler_params=_params("parallel"),
        name="ffn_ln",
    )(h, w1, w2, g, b)


def _gmlp_kernel(x_ref, win_ref, vng_ref, vnb_ref, ws_ref, bs_ref, wout_ref, g_ref, b_ref, o_ref,
                 *, alpha, parts):
    tm = x_ref.shape[0]
    gw = wout_ref.shape[0]
    gd = gw // GMLP_GROUPS
    pm = tm // parts
    xs = [x_ref[i * pm:(i + 1) * pm] for i in range(parts)]
    zs = [jnp.dot(xs[i].astype(BF16), win_ref[...], preferred_element_type=F32)
          for i in range(parts)]
    for i in range(parts):
        z = zs[i]
        z = 0.5 * z * (1.0 + lax.erf(z * math.sqrt(0.5)))
        u = z[:, :gw]
        v = _layer_norm(z[:, gw:], vng_ref[...], vnb_ref[...], LN_EPS).astype(BF16)
        rows = []
        for c in range(pm // GMLP_CHUNK):
            cols = []
            for g in range(GMLP_GROUPS):
                vb = v[c * GMLP_CHUNK:(c + 1) * GMLP_CHUNK, g * gd:(g + 1) * gd]
                cols.append(jnp.dot(ws_ref[g], vb, preferred_element_type=F32) + bs_ref[g])
            rows.append(jnp.concatenate(cols, axis=1))
        mixed = jnp.concatenate(rows, axis=0)
        mix = jnp.dot((u * mixed).astype(BF16), wout_ref[...], preferred_element_type=F32)
        o_ref[i * pm:(i + 1) * pm] = _layer_norm(alpha * xs[i] + mix, g_ref[...], b_ref[...],
                                                  LN_EPS)


def _gmlp(x, w_in, vn_g, vn_b, ws, bs, w_out, g, b, alpha, tm=1024, parts=4):
    t, d = x.shape
    gw = w_out.shape[0]
    return pl.pallas_call(
        functools.partial(_gmlp_kernel, alpha=alpha, parts=parts),
        grid=(t // tm,),
        in_specs=[pl.BlockSpec((tm, d), lambda i: (i, 0)),
                  _const_spec((d, 2 * gw)), _const_spec((1, gw)), _const_spec((1, gw)),
                  _const_spec(ws.shape), _const_spec(bs.shape), _const_spec((gw, d)),
                  _const_spec((1, d)), _const_spec((1, d))],
        out_specs=pl.BlockSpec((tm, d), lambda i: (i, 0)),
        out_shape=jax.ShapeDtypeStruct((t, d), F32),
        compiler_params=_params("parallel"),
        name="gmlp_ln",
    )(x, w_in, vn_g, vn_b, ws, bs, w_out, g, b)


def _rwkv_proj_kernel(*refs, seq_tiles, has_vres):
    if has_vres:
        (x_ref, xp_ref, mu_ref, wr_ref, wk_ref, wv_ref, w0_ref, w1_ref, w2_ref, a0_ref, a1_ref,
         a2_ref, g1_ref, g2_ref, vf_ref, v0_ref, v1_ref, v2_ref,
         r_out, lw_out, k_out, v_out, a_out, g_out) = refs
    else:
        (x_ref, xp_ref, mu_ref, wr_ref, wk_ref, wv_ref, w0_ref, w1_ref, w2_ref, a0_ref, a1_ref,
         a2_ref, g1_ref, g2_ref,
         r_out, lw_out, k_out, v_out, a_out, g_out) = refs
    x = x_ref[...]
    not_first = (pl.program_id(0) % seq_tiles != 0).astype(F32)
    prev = xp_ref[7:8, :] * not_first
    row = lax.broadcasted_iota(jnp.int32, x.shape, 0)
    shifted = jnp.where(row == 0, prev, pltpu.roll(x, shift=1, axis=0))
    xx = shifted - x
    mix = lambda i: (x + xx * mu_ref[i:i + 1, :]).astype(BF16)

    r_out[...] = _bdot(mix(0), wr_ref[...])
    w_in = w0_ref[...] + _bdot(jnp.tanh(_bdot(mix(1), w1_ref[...])), w2_ref[...])
    k_out[...] = _bdot(mix(2), wk_ref[...])
    lw_out[...] = -math.exp(-0.5) * jax.nn.sigmoid(w_in)
    xv = mix(3)
    v = _bdot(xv, wv_ref[...])
    a_in = a0_ref[...] + _bdot(_bdot(mix(4), a1_ref[...]), a2_ref[...])
    if has_vres:
        gate_in = v0_ref[...] + _bdot(_bdot(xv, v1_ref[...]), v2_ref[...])
    g_hidden = _bdot(mix(5), g1_ref[...])
    a_out[...] = jax.nn.sigmoid(a_in)
    if has_vres:
        v = v + (vf_ref[...] - v) * jax.nn.sigmoid(gate_in)
    v_out[...] = v
    g_out[...] = _bdot(jax.nn.sigmoid(g_hidden), g2_ref[...])


def _rwkv_proj(h, seq, weights, vres, tm=512):
    t, d = h.shape
    has_vres = vres is not None
    row_spec = pl.BlockSpec((tm, d), lambda i: (i, 0))
    prev_spec = pl.BlockSpec((8, d), lambda i: (jnp.maximum(i * (tm // 8) - 1, 0), 0))
    args = [h, h] + list(weights)
    in_specs = [row_spec, prev_spec] + [_const_spec(w.shape) for w in weights]
    if has_vres:
        v_first, v0, v1, v2 = vres
        args += [v_first, v0, v1, v2]
        in_specs += [row_spec, _const_spec(v0.shape), _const_spec(v1.shape), _const_spec(v2.shape)]
    return pl.pallas_call(
        functools.partial(_rwkv_proj_kernel, seq_tiles=seq // tm, has_vres=has_vres),
        grid=(t // tm,),
        in_specs=in_specs,
        out_specs=[row_spec] * 6,
        out_shape=[jax.ShapeDtypeStruct((t, d), F32)] * 6,
        compiler_params=_params("parallel"),
        name="rwkv_proj",
    )(*args)


def _quarters(x, quarter_masks):
    zero = jnp.zeros_like(x)
    return jnp.concatenate([jnp.where(m, x, zero) for m in quarter_masks], axis=0)


def _head_transpose(x, lo_mask):
    halves = []
    for s in range(x.shape[1] // LANES):
        xs = x[:, s * LANES:(s + 1) * LANES]
        t = jnp.concatenate([xs, xs], axis=0).T
        halves.append(jnp.where(lo_mask, t[:RWKV_HEAD], t[RWKV_HEAD:]))
    return jnp.concatenate(halves, axis=1)


def _wkv_kernel(r_ref, lw_ref, k_ref, v_ref, a_ref, kk_ref, ka_ref, rk_ref, gg_ref, gb_ref,
                y_ref, h_ref, rt_s, nkt_s, bt_s, kk_s, vb_s, bonus_s, glast_s, *, seq_tiles):
    ts, d = r_ref.shape
    pairs = d // LANES
    half = pairs // 2
    rows = half * ts
    W = MXU_DIM
    L = WKV_CHUNK
    per_pair = ts // L
    n_chunks = half * per_pair
    step = pl.program_id(0)

    @pl.when(step == 0)
    def _():
        for ref in (h_ref, rt_s, nkt_s, bt_s, kk_s, vb_s, bonus_s, glast_s):
            ref[...] = jnp.zeros_like(ref)

    def slot_cols(x, i):
        return jnp.concatenate([x[:, i * LANES:(i + 1) * LANES],
                                x[:, (i + half) * LANES:(i + half + 1) * LANES]], axis=1)

    def per_row(ref):
        p = ref[...]
        return jnp.concatenate([jnp.broadcast_to(slot_cols(p, i), (ts, W)) for i in range(half)],
                               axis=0)

    li = lax.broadcasted_iota(jnp.int32, (L, W), 0)
    lj = lax.broadcasted_iota(jnp.int32, (L, W), 1)
    quarter_masks = [(lj // RWKV_HEAD) == q for q in range(W // RWKV_HEAD)]
    src = lj % RWKV_HEAD
    strict = li > src
    incl = li >= src
    eye = (li == src).astype(F32)
    lo_mask = lax.broadcasted_iota(jnp.int32, (L, LANES), 1) < RWKV_HEAD
    qi = lax.broadcasted_iota(jnp.int32, (W, W), 0)
    qj = lax.broadcasted_iota(jnp.int32, (W, W), 1)
    head_ones = ((qi // RWKV_HEAD) == (qj // RWKV_HEAD)).astype(BF16)
    ti = lax.broadcasted_iota(jnp.int32, (LANES, LANES), 0)
    tj = lax.broadcasted_iota(jnp.int32, (LANES, LANES), 1)
    tri = ((ti >= tj) & (ti // L == tj // L)).astype(BF16)

    chunks = range(n_chunks)
    bd = lambda x: _quarters(x.astype(BF16), quarter_masks)
    mm = lambda x, w: jnp.dot(x.astype(BF16), w, preferred_element_type=F32)
    bonus_prev = bonus_s[...]
    rt = [rt_s[c * L:(c + 1) * L] for c in chunks]
    nkt = [nkt_s[c * L:(c + 1) * L] for c in chunks]
    bt = [bt_s[c * L:(c + 1) * L] for c in chunks]
    kk_ = [kk_s[c * L:(c + 1) * L] for c in chunks]
    bd_v = [bd(vb_s[c * L:(c + 1) * L]) for c in chunks]
    g_last = [glast_s[c:c + 1, :] for c in chunks]

    att = [lax.dot_general(jnp.concatenate([nkt[c], rt[c].astype(BF16)], axis=0),
                           jnp.concatenate([bd(bt[c]), bd(kk_[c])], axis=0),
                           (((1,), (1,)), ((), ())), preferred_element_type=F32) for c in chunks]
    q = [jnp.where(strict, att[c][:L, :W], 0.0) for c in chunks]
    nak = [jnp.where(strict, att[c][:L, W:], 0.0).astype(BF16) for c in chunks]
    a_rb = [jnp.where(incl, att[c][L:, :W], 0.0).astype(BF16) for c in chunks]
    a_rk = [jnp.where(incl, att[c][L:, W:], 0.0).astype(BF16) for c in chunks]
    nav = [mm(nak[c], bd_v[c]) for c in chunks]
    be_t = [_head_transpose(bt[c] * g_last[c], lo_mask).astype(BF16) for c in chunks]
    ke_t = [_head_transpose(kk_[c] * g_last[c], lo_mask).astype(BF16) for c in chunks]
    g_key = []
    for c in chunks:
        sides = []
        for s in range(W // LANES):
            t = jnp.broadcast_to(g_last[c][:, s * LANES:(s + 1) * LANES], (LANES, LANES)).T
            sides.append(jnp.where(lo_mask, t[:RWKV_HEAD], t[RWKV_HEAD:]))
        g_key.append(jnp.concatenate(sides, axis=1))

    def prepare(i):
        slot = lambda ref: jnp.concatenate(
            [ref[:, i * LANES:(i + 1) * LANES],
             ref[:, (i + half) * LANES:(i + half + 1) * LANES]], axis=1)
        sl = slice(i * ts, (i + 1) * ts)
        r, lw, k, v, a = slot(r_ref), slot(lw_ref), slot(k_ref), slot(v_ref), slot(a_ref)
        kraw = k * slot(kk_ref)
        k2 = k * (1.0 + (a - 1.0) * slot(ka_ref))
        sums = _bdot(jnp.concatenate([kraw * kraw, r * k2 * slot(rk_ref)], axis=0), head_ones)
        kappa = kraw / jnp.maximum(jnp.sqrt(sums[:ts]), 1e-12)
        cs = jnp.concatenate([_dot_exact_lhs(tri, lw[j * LANES:(j + 1) * LANES])
                              for j in range(ts // LANES)], axis=0)
        g_incl = jnp.exp(cs)
        g_inv = jnp.exp(-cs)
        rt_s[sl] = r * g_incl
        nkt_s[sl] = (-kappa * jnp.exp(cs - lw)).astype(BF16)
        bt_s[sl] = kappa * a * g_inv
        kk_s[sl] = k2 * g_inv
        vb_s[sl] = v.astype(BF16)
        bonus_s[sl] = sums[ts:] * v
        glast_s[i * per_pair:(i + 1) * per_pair] = jnp.concatenate(
            [g_incl[(j + 1) * L - 1:(j + 1) * L, :] for j in range(per_pair)], axis=0)

    p = [eye + q[c] for c in chunks]
    qb = [q[c].astype(BF16) for c in chunks]
    q = [mm(qb[c], bd(qb[c])) for c in chunks]
    doublings = int(math.log2(L)) - 2
    assert half == doublings, "one prepare() slot per doubling step"
    for i in range(doublings):
        prepare(i)
        qb = [q[c].astype(BF16) for c in chunks]
        pq = [mm(jnp.concatenate([qb[c], p[c].astype(BF16)], axis=0), bd(qb[c])) for c in chunks]
        q = [pq[c][:L] for c in chunks]
        p = [p[c] + pq[c][L:] for c in chunks]
    p = [p[c] + mm(p[c], bd(q[c])) for c in chunks]
    tw = [mm(p[c], jnp.concatenate([bd(nkt[c]), bd(nav[c])], axis=1)) for c in chunks]
    nw = [bd(tw[c][:, :W]) for c in chunks]
    bd_uv = [jnp.concatenate([bd(tw[c][:, W:]), bd_v[c]], axis=0) for c in chunks]
    yg = [mm(jnp.concatenate([jnp.concatenate([a_rb[c], a_rk[c]], axis=1),
                              jnp.concatenate([be_t[c], ke_t[c]], axis=1)], axis=0), bd_uv[c])
          for c in chunks]
    rm = [mm(jnp.concatenate([a_rb[c], be_t[c]], axis=0), nw[c]) for c in chunks]
    lhs = [jnp.concatenate([rt[c] + rm[c][:L], rm[c][L:]], axis=0).astype(BF16) for c in chunks]

    seq_start = (step + seq_tiles - 1) % seq_tiles == 0
    hs = [jnp.where(seq_start, 0.0, h_ref[p_]) for p_ in range(half)]
    y_chunks = [None] * n_chunks
    for j in range(per_pair):
        for p_ in range(half):
            c = p_ * per_pair + j
            prod = mm(lhs[c], bd(hs[p_]))
            y_chunks[c] = yg[c][:L] + prod[:L]
            hs[p_] = g_key[c] * hs[p_] + prod[L:] + yg[c][L:]
    for p_ in range(half):
        h_ref[p_] = hs[p_]

    y = jnp.concatenate(y_chunks, axis=0)
    inv_n = 1.0 / RWKV_HEAD
    mean = _dot2_rhs(y, head_ones) * inv_n
    yc = y - mean
    var = _bdot(yc * yc, head_ones) * inv_n
    yn = yc * lax.rsqrt(var + GN_EPS) * per_row(gg_ref) + per_row(gb_ref)
    out = yn + bonus_prev
    for i in range(half):
        y_ref[:, i * LANES:(i + 1) * LANES] = out[i * ts:(i + 1) * ts, :LANES]
        y_ref[:, (i + half) * LANES:(i + half + 1) * LANES] = out[i * ts:(i + 1) * ts, LANES:]


def _wkv(r, lw, k, v, a, head_params, seq, ts=256):
    t, d = r.shape
    tiles = t // ts
    rows = (d // MXU_DIM) * ts
    in_spec = pl.BlockSpec((ts, d), lambda n: (jnp.minimum(n, tiles - 1), 0))
    out_spec = pl.BlockSpec((ts, d), lambda n: (jnp.maximum(n - 1, 0), 0))
    stage_f32 = pltpu.VMEM((rows, MXU_DIM), F32)
    stage_bf16 = pltpu.VMEM((rows, MXU_DIM), BF16)
    return pl.pallas_call(
        functools.partial(_wkv_kernel, seq_tiles=seq // ts),
        grid=(tiles + 1,),
        in_specs=[in_spec] * 5 + [_const_spec((1, d))] * len(head_params),
        out_specs=out_spec,
        out_shape=jax.ShapeDtypeStruct((t, d), F32),
        scratch_shapes=[pltpu.VMEM((d // MXU_DIM, RWKV_HEAD, MXU_DIM), F32),
                        stage_f32, stage_bf16, stage_f32, stage_f32, stage_bf16, stage_f32,
                        pltpu.VMEM((rows // WKV_CHUNK, MXU_DIM), F32)],
        compiler_params=_params("arbitrary"),
        name="wkv7_chunked",
    )(r, lw, k, v, a, *head_params)


def kernel(x, ln1_g, ln1_b, ln2_g, ln2_b, ffn_w1, ffn_w2, gm_w_in, gm_vn_g, gm_vn_b, gm_ws, gm_bs, gm_w_out, rw_mu, rw_w_rkv, rw_w0, rw_w1, rw_w2, rw_a0, rw_a1, rw_a2, rw_g1, rw_g2, rw_k_k, rw_k_a, rw_r_k, rw_gn_g, rw_gn_b, rw_w_o, rw_v0, rw_v1, rw_v2):
    batch, seq, d = x.shape
    depth = ln1_g.shape[0]
    alpha = (2.0 * depth) ** 0.25
    h = x.reshape(batch * seq, d)
    row = lambda vec: vec.reshape(1, -1)
    causal = jnp.tril(jnp.ones((GMLP_CHUNK, GMLP_CHUNK), dtype=bool))
    v_first = None
    for i in range(depth):
        j = i // 2
        if i % 2 == 0:
            ws = jnp.where(causal[None], gm_ws[j], 0.0).astype(BF16)
            bs = jnp.broadcast_to(gm_bs[j][:, :, None], gm_bs[j].shape + (d // GMLP_GROUPS,))
            h = _gmlp(h, gm_w_in[j].astype(BF16), row(gm_vn_g[j]), row(gm_vn_b[j]), ws, bs,
                      gm_w_out[j].astype(BF16), row(ln1_g[i]), row(ln1_b[i]), alpha)
            h = _ffn(h, ffn_w1[i].astype(BF16), ffn_w2[i].astype(BF16), row(ln2_g[i]),
                     row(ln2_b[i]), alpha)
        else:
            vres = None
            if j > 0:
                vres = (v_first, row(rw_v0[j - 1]), rw_v1[j - 1].astype(BF16),
                        rw_v2[j - 1].astype(BF16))
            proj_weights = (rw_mu[j], rw_w_rkv[j, 0].astype(BF16), rw_w_rkv[j, 1].astype(BF16),
                            rw_w_rkv[j, 2].astype(BF16), row(rw_w0[j]), rw_w1[j].astype(BF16),
                            rw_w2[j].astype(BF16), row(rw_a0[j]), rw_a1[j].astype(BF16),
                            rw_a2[j].astype(BF16), rw_g1[j].astype(BF16), rw_g2[j].astype(BF16))
            head_params = (row(rw_k_k[j]), row(rw_k_a[j]), row(rw_r_k[j]), row(rw_gn_g[j]),
                           row(rw_gn_b[j]))
            r, lw, k, v, a, g = _rwkv_proj(h, seq, proj_weights, vres)
            if v_first is None:
                v_first = v
            y = _wkv(r, lw, k, v, a, head_params, seq)
            h = _out_ffn(y, g, h, rw_w_o[j].astype(BF16), row(ln1_g[i]), row(ln1_b[i]),
                         ffn_w1[i].astype(BF16), ffn_w2[i].astype(BF16), row(ln2_g[i]),
                         row(ln2_b[i]), alpha)
    return h.reshape(batch, seq, d)
```

```python
import functools
import math

import jax
import jax.numpy as jnp
from jax import lax
from jax.experimental import pallas as pl
from jax.experimental.pallas import tpu as pltpu

F32 = jnp.float32
BF16 = jnp.bfloat16

GMLP_CHUNK = 128
GMLP_GROUPS = 8
RWKV_HEAD = 64
LN_EPS = 1e-5
GN_EPS = 64e-5

LANES = 128
MXU_DIM = 256
WKV_CHUNK = 64
VMEM_LIMIT = 56 * 1024 * 1024


def _bdot(a, b):
    return jnp.dot(a.astype(BF16), b.astype(BF16), preferred_element_type=F32)


def _split3(x):
    hi = x.astype(BF16)
    r1 = x - hi.astype(F32)
    mid = r1.astype(BF16)
    lo = (r1 - mid.astype(F32)).astype(BF16)
    return hi, mid, lo


def _dot_exact_lhs(a_bf16, x):
    hi, mid, lo = _split3(x)
    out = jnp.dot(a_bf16, hi, preferred_element_type=F32)
    out += jnp.dot(a_bf16, mid, preferred_element_type=F32)
    out += jnp.dot(a_bf16, lo, preferred_element_type=F32)
    return out


def _dot2_rhs(x, b_bf16):
    hi = x.astype(BF16)
    lo = (x - hi.astype(F32)).astype(BF16)
    n = x.shape[0]
    both = jnp.dot(jnp.concatenate([hi, lo], axis=0), b_bf16, preferred_element_type=F32)
    return both[:n] + both[n:]


def _layer_norm(x, g, b, eps):
    mean = jnp.mean(x, axis=-1, keepdims=True)
    xc = x - mean
    var = jnp.mean(xc * xc, axis=-1, keepdims=True)
    return xc * lax.rsqrt(var + eps) * g + b


def _const_spec(shape):
    nd = len(shape)
    return pl.BlockSpec(shape, lambda *_: (0,) * nd, pipeline_mode=pl.Buffered(1))


def _pick(stacked, *idx):
    return (stacked, idx)


def _wshape(w):
    return w[0].shape[len(w[1]):] if isinstance(w, tuple) else w.shape


def _warg(w):
    return w[0] if isinstance(w, tuple) else w


def _wspec(w):
    if not isinstance(w, tuple):
        return _const_spec(w.shape)
    idx, rest = w[1], _wshape(w)
    return pl.BlockSpec((None,) * len(idx) + rest, lambda *_: idx + (0,) * len(rest),
                        pipeline_mode=pl.Buffered(1))


def _params(*sem):
    return pltpu.CompilerParams(dimension_semantics=sem, vmem_limit_bytes=VMEM_LIMIT)


def _ffn_body(h, w1_ref, w2_ref, g_ref, b_ref, alpha, ff_block):
    hb = h.astype(BF16)
    d_ff = w1_ref.shape[1]
    acc = alpha * h
    for j in range(d_ff // ff_block):
        a = jnp.dot(hb, w1_ref[:, j * ff_block:(j + 1) * ff_block], preferred_element_type=F32)
        a = jnp.maximum(a, 0.0)
        a = (a * a).astype(BF16)
        acc += jnp.dot(a, w2_ref[j * ff_block:(j + 1) * ff_block, :], preferred_element_type=F32)
    return _layer_norm(acc, g_ref[...], b_ref[...], LN_EPS)


def _ffn_kernel(h_ref, w1_ref, w2_ref, g_ref, b_ref, o_ref, *, alpha, ff_block):
    o_ref[...] = _ffn_body(h_ref[...], w1_ref, w2_ref, g_ref, b_ref, alpha, ff_block)


def _out_ffn_kernel(y_ref, g_ref, h_ref, wo_ref, l1g_ref, l1b_ref, w1_ref, w2_ref, l2g_ref, l2b_ref,
                    o_ref, *, alpha, ff_block, parts):
    pm = h_ref.shape[0] // parts
    rows = [slice(i * pm, (i + 1) * pm) for i in range(parts)]
    mixes = [jnp.dot((y_ref[r] * g_ref[r]).astype(BF16), wo_ref[...], preferred_element_type=F32)
             for r in rows]
    h1 = jnp.concatenate([_layer_norm(alpha * h_ref[r] + m, l1g_ref[...], l1b_ref[...], LN_EPS)
                          for r, m in zip(rows, mixes)], axis=0)
    o_ref[...] = _ffn_body(h1, w1_ref, w2_ref, l2g_ref, l2b_ref, alpha, ff_block)


def _out_ffn(y, g, h, w_o, ln1_g, ln1_b, w1, w2, ln2_g, ln2_b, alpha, tm=512, ff_block=1024,
             parts=2):
    t, d = h.shape
    row_spec = pl.BlockSpec((tm, d), lambda i: (i, 0))
    return pl.pallas_call(
        functools.partial(_out_ffn_kernel, alpha=alpha, ff_block=ff_block, parts=parts),
        grid=(t // tm,),
        in_specs=[row_spec, row_spec, row_spec, _wspec(w_o), _const_spec((1, d)),
                  _const_spec((1, d)), _wspec(w1), _wspec(w2),
                  _const_spec((1, d)), _const_spec((1, d))],
        out_specs=row_spec,
        out_shape=jax.ShapeDtypeStruct((t, d), F32),
        compiler_params=_params("parallel"),
        name="rwkv_out_ffn_ln",
    )(y, g, h, _warg(w_o), ln1_g, ln1_b, _warg(w1), _warg(w2), ln2_g, ln2_b)


def _ffn(h, w1, w2, g, b, alpha, tm=1024, ff_block=1024):
    t, d = h.shape
    return pl.pallas_call(
        functools.partial(_ffn_kernel, alpha=alpha, ff_block=ff_block),
        grid=(t // tm,),
        in_specs=[pl.BlockSpec((tm, d), lambda i: (i, 0)), _wspec(w1), _wspec(w2),
                  _const_spec((1, d)), _const_spec((1, d))],
        out_specs=pl.BlockSpec((tm, d), lambda i: (i, 0)),
        out_shape=jax.ShapeDtypeStruct((t, d), F32),
        compiler_params=_params("parallel"),
        name="ffn_ln",
    )(h, _warg(w1), _warg(w2), g, b)


def _gmlp_kernel(x_ref, win_ref, vng_ref, vnb_ref, ws_ref, bs_ref, wout_ref, g_ref, b_ref, o_ref,
                 *, alpha, parts):
    tm = x_ref.shape[0]
    gw = wout_ref.shape[0]
    gd = gw // GMLP_GROUPS
    pm = tm // parts
    xs = [x_ref[i * pm:(i + 1) * pm] for i in range(parts)]
    zs = [jnp.dot(xs[i].astype(BF16), win_ref[...], preferred_element_type=F32)
          for i in range(parts)]
    for i in range(parts):
        z = zs[i]
        z = 0.5 * z * (1.0 + lax.erf(z * math.sqrt(0.5)))
        u = z[:, :gw]
        v = _layer_norm(z[:, gw:], vng_ref[...], vnb_ref[...], LN_EPS).astype(BF16)
        rows = []
        for c in range(pm // GMLP_CHUNK):
            cols = []
            for g in range(GMLP_GROUPS):
                vb = v[c * GMLP_CHUNK:(c + 1) * GMLP_CHUNK, g * gd:(g + 1) * gd]
                cols.append(jnp.dot(ws_ref[g], vb, preferred_element_type=F32) + bs_ref[g])
            rows.append(jnp.concatenate(cols, axis=1))
        mixed = jnp.concatenate(rows, axis=0)
        mix = jnp.dot((u * mixed).astype(BF16), wout_ref[...], preferred_element_type=F32)
        o_ref[i * pm:(i + 1) * pm] = _layer_norm(alpha * xs[i] + mix, g_ref[...], b_ref[...],
                                                  LN_EPS)


def _gmlp(x, w_in, vn_g, vn_b, ws, bs, w_out, g, b, alpha, tm=1024, parts=4):
    t, d = x.shape
    gw = _wshape(w_out)[0]
    return pl.pallas_call(
        functools.partial(_gmlp_kernel, alpha=alpha, parts=parts),
        grid=(t // tm,),
        in_specs=[pl.BlockSpec((tm, d), lambda i: (i, 0)),
                  _wspec(w_in), _const_spec((1, gw)), _const_spec((1, gw)),
                  _const_spec(ws.shape), _const_spec(bs.shape), _wspec(w_out),
                  _const_spec((1, d)), _const_spec((1, d))],
        out_specs=pl.BlockSpec((tm, d), lambda i: (i, 0)),
        out_shape=jax.ShapeDtypeStruct((t, d), F32),
        compiler_params=_params("parallel"),
        name="gmlp_ln",
    )(x, _warg(w_in), vn_g, vn_b, ws, bs, _warg(w_out), g, b)


def _rwkv_proj_kernel(*refs, seq_tiles, has_vres):
    if has_vres:
        (x_ref, xp_ref, mu_ref, wr_ref, wk_ref, wv_ref, w0_ref, w1_ref, w2_ref, a0_ref, a1_ref,
         a2_ref, g1_ref, g2_ref, vf_ref, v0_ref, v1_ref, v2_ref,
         r_out, lw_out, k_out, v_out, a_out, g_out) = refs
    else:
        (x_ref, xp_ref, mu_ref, wr_ref, wk_ref, wv_ref, w0_ref, w1_ref, w2_ref, a0_ref, a1_ref,
         a2_ref, g1_ref, g2_ref,
         r_out, lw_out, k_out, v_out, a_out, g_out) = refs
    x = x_ref[...]
    not_first = (pl.program_id(0) % seq_tiles != 0).astype(F32)
    prev = xp_ref[7:8, :] * not_first
    row = lax.broadcasted_iota(jnp.int32, x.shape, 0)
    shifted = jnp.where(row == 0, prev, pltpu.roll(x, shift=1, axis=0))
    xx = shifted - x
    mix = lambda i: (x + xx * mu_ref[i:i + 1, :]).astype(BF16)

    r_out[...] = _bdot(mix(0), wr_ref[...])
    w_in = w0_ref[...] + _bdot(jnp.tanh(_bdot(mix(1), w1_ref[...])), w2_ref[...])
    k_out[...] = _bdot(mix(2), wk_ref[...])
    lw_out[...] = -math.exp(-0.5) * jax.nn.sigmoid(w_in)
    xv = mix(3)
    v = _bdot(xv, wv_ref[...])
    a_in = a0_ref[...] + _bdot(_bdot(mix(4), a1_ref[...]), a2_ref[...])
    if has_vres:
        gate_in = v0_ref[...] + _bdot(_bdot(xv, v1_ref[...]), v2_ref[...])
    g_hidden = _bdot(mix(5), g1_ref[...])
    a_out[...] = jax.nn.sigmoid(a_in)
    if has_vres:
        v = v + (vf_ref[...] - v) * jax.nn.sigmoid(gate_in)
    v_out[...] = v
    g_out[...] = _bdot(jax.nn.sigmoid(g_hidden), g2_ref[...])


def _rwkv_proj(h, seq, weights, vres, tm=512):
    t, d = h.shape
    has_vres = vres is not None
    row_spec = pl.BlockSpec((tm, d), lambda i: (i, 0))
    prev_spec = pl.BlockSpec((8, d), lambda i: (jnp.maximum(i * (tm // 8) - 1, 0), 0))
    args = [h, h] + [_warg(w) for w in weights]
    in_specs = [row_spec, prev_spec] + [_wspec(w) for w in weights]
    if has_vres:
        v_first, v0, v1, v2 = vres
        args += [v_first, v0, v1, v2]
        in_specs += [row_spec, _const_spec(v0.shape), _const_spec(v1.shape), _const_spec(v2.shape)]
    return pl.pallas_call(
        functools.partial(_rwkv_proj_kernel, seq_tiles=seq // tm, has_vres=has_vres),
        grid=(t // tm,),
        in_specs=in_specs,
        out_specs=[row_spec] * 6,
        out_shape=[jax.ShapeDtypeStruct((t, d), F32)] * 6,
        compiler_params=_params("parallel"),
        name="rwkv_proj",
    )(*args)


def _quarters(x, quarter_masks):
    zero = jnp.zeros_like(x)
    return jnp.concatenate([jnp.where(m, x, zero) for m in quarter_masks], axis=0)


def _head_transpose(x, lo_mask):
    halves = []
    for s in range(x.shape[1] // LANES):
        xs = x[:, s * LANES:(s + 1) * LANES]
        t = jnp.concatenate([xs, xs], axis=0).T
        halves.append(jnp.where(lo_mask, t[:RWKV_HEAD], t[RWKV_HEAD:]))
    return jnp.concatenate(halves, axis=1)


def _wkv_kernel(r_ref, lw_ref, k_ref, v_ref, a_ref, kk_ref, ka_ref, rk_ref, gg_ref, gb_ref,
                y_ref, h_ref, rt_s, nkt_s, bt_s, kk_s, vb_s, bonus_s, glast_s, *, seq_tiles):
    ts, d = r_ref.shape
    pairs = d // LANES
    half = pairs // 2
    rows = half * ts
    W = MXU_DIM
    L = WKV_CHUNK
    per_pair = ts // L
    n_chunks = half * per_pair
    step = pl.program_id(0)

    @pl.when(step == 0)
    def _():
        for ref in (h_ref, rt_s, nkt_s, bt_s, kk_s, vb_s, bonus_s, glast_s):
            ref[...] = jnp.zeros_like(ref)

    def slot_cols(x, i):
        return jnp.concatenate([x[:, i * LANES:(i + 1) * LANES],
                                x[:, (i + half) * LANES:(i + half + 1) * LANES]], axis=1)

    def per_row(ref):
        p = ref[...]
        return jnp.concatenate([jnp.broadcast_to(slot_cols(p, i), (ts, W)) for i in range(half)],
                               axis=0)

    li = lax.broadcasted_iota(jnp.int32, (L, W), 0)
    lj = lax.broadcasted_iota(jnp.int32, (L, W), 1)
    quarter_masks = [(lj // RWKV_HEAD) == q for q in range(W // RWKV_HEAD)]
    src = lj % RWKV_HEAD
    strict = li > src
    incl = li >= src
    eye = (li == src).astype(F32)
    lo_mask = lax.broadcasted_iota(jnp.int32, (L, LANES), 1) < RWKV_HEAD
    qi = lax.broadcasted_iota(jnp.int32, (W, W), 0)
    qj = lax.broadcasted_iota(jnp.int32, (W, W), 1)
    head_ones = ((qi // RWKV_HEAD) == (qj // RWKV_HEAD)).astype(BF16)
    ti = lax.broadcasted_iota(jnp.int32, (LANES, LANES), 0)
    tj = lax.broadcasted_iota(jnp.int32, (LANES, LANES), 1)
    tri = ((ti >= tj) & (ti // L == tj // L)).astype(BF16)

    chunks = range(n_chunks)
    bd = lambda x: _quarters(x.astype(BF16), quarter_masks)
    mm = lambda x, w: jnp.dot(x.astype(BF16), w, preferred_element_type=F32)
    bonus_prev = bonus_s[...]
    rt = [rt_s[c * L:(c + 1) * L] for c in chunks]
    nkt = [nkt_s[c * L:(c + 1) * L] for c in chunks]
    bt = [bt_s[c * L:(c + 1) * L] for c in chunks]
    kk_ = [kk_s[c * L:(c + 1) * L] for c in chunks]
    bd_v = [bd(vb_s[c * L:(c + 1) * L]) for c in chunks]
    g_last = [glast_s[c:c + 1, :] for c in chunks]

    att = [lax.dot_general(jnp.concatenate([nkt[c], rt[c].astype(BF16)], axis=0),
                           jnp.concatenate([bd(bt[c]), bd(kk_[c])], axis=0),
                           (((1,), (1,)), ((), ())), preferred_element_type=F32) for c in chunks]
    q = [jnp.where(strict, att[c][:L, :W], 0.0) for c in chunks]
    nak = [jnp.where(strict, att[c][:L, W:], 0.0).astype(BF16) for c in chunks]
    a_rb = [jnp.where(incl, att[c][L:, :W], 0.0).astype(BF16) for c in chunks]
    a_rk = [jnp.where(incl, att[c][L:, W:], 0.0).astype(BF16) for c in chunks]
    nav = [mm(nak[c], bd_v[c]) for c in chunks]
    be_t = [_head_transpose(bt[c] * g_last[c], lo_mask).astype(BF16) for c in chunks]
    ke_t = [_head_transpose(kk_[c] * g_last[c], lo_mask).astype(BF16) for c in chunks]
    g_key = []
    for c in chunks:
        sides = []
        for s in range(W // LANES):
            t = jnp.broadcast_to(g_last[c][:, s * LANES:(s + 1) * LANES], (LANES, LANES)).T
            sides.append(jnp.where(lo_mask, t[:RWKV_HEAD], t[RWKV_HEAD:]))
        g_key.append(jnp.concatenate(sides, axis=1))

    def prepare(i):
        slot = lambda ref: jnp.concatenate(
            [ref[:, i * LANES:(i + 1) * LANES],
             ref[:, (i + half) * LANES:(i + half + 1) * LANES]], axis=1)
        sl = slice(i * ts, (i + 1) * ts)
        r, lw, k, v, a = slot(r_ref), slot(lw_ref), slot(k_ref), slot(v_ref), slot(a_ref)
        kraw = k * slot(kk_ref)
        k2 = k * (1.0 + (a - 1.0) * slot(ka_ref))
        sums = _bdot(jnp.concatenate([kraw * kraw, r * k2 * slot(rk_ref)], axis=0), head_ones)
        kappa = kraw / jnp.maximum(jnp.sqrt(sums[:ts]), 1e-12)
        cs = jnp.concatenate([_dot_exact_lhs(tri, lw[j * LANES:(j + 1) * LANES])
                              for j in range(ts // LANES)], axis=0)
        g_incl = jnp.exp(cs)
        g_inv = jnp.exp(-cs)
        rt_s[sl] = r * g_incl
        nkt_s[sl] = (-kappa * jnp.exp(cs - lw)).astype(BF16)
        bt_s[sl] = kappa * a * g_inv
        kk_s[sl] = k2 * g_inv
        vb_s[sl] = v.astype(BF16)
        bonus_s[sl] = sums[ts:] * v
        glast_s[i * per_pair:(i + 1) * per_pair] = jnp.concatenate(
            [g_incl[(j + 1) * L - 1:(j + 1) * L, :] for j in range(per_pair)], axis=0)

    p = [eye + q[c] for c in chunks]
    qb = [q[c].astype(BF16) for c in chunks]
    q = [mm(qb[c], bd(qb[c])) for c in chunks]
    doublings = int(math.log2(L)) - 2
    assert half == doublings, "one prepare() slot per doubling step"
    for i in range(doublings):
        prepare(i)
        qb = [q[c].astype(BF16) for c in chunks]
        pq = [mm(jnp.concatenate([qb[c], p[c].astype(BF16)], axis=0), bd(qb[c])) for c in chunks]
        q = [pq[c][:L] for c in chunks]
        p = [p[c] + pq[c][L:] for c in chunks]
    p = [p[c] + mm(p[c], bd(q[c])) for c in chunks]
    tw = [mm(p[c], jnp.concatenate([bd(nkt[c]), bd(nav[c])], axis=1)) for c in chunks]
    nw = [bd(tw[c][:, :W]) for c in chunks]
    bd_uv = [jnp.concatenate([bd(tw[c][:, W:]), bd_v[c]], axis=0) for c in chunks]
    yg = [mm(jnp.concatenate([jnp.concatenate([a_rb[c], a_rk[c]], axis=1),
                              jnp.concatenate([be_t[c], ke_t[c]], axis=1)], axis=0), bd_uv[c])
          for c in chunks]
    rm = [mm(jnp.concatenate([a_rb[c], be_t[c]], axis=0), nw[c]) for c in chunks]
    lhs = [jnp.concatenate([rt[c] + rm[c][:L], rm[c][L:]], axis=0).astype(BF16) for c in chunks]

    seq_start = (step + seq_tiles - 1) % seq_tiles == 0
    hs = [jnp.where(seq_start, 0.0, h_ref[p_]) for p_ in range(half)]
    y_chunks = [None] * n_chunks
    for j in range(per_pair):
        for p_ in range(half):
            c = p_ * per_pair + j
            prod = mm(lhs[c], bd(hs[p_]))
            y_chunks[c] = yg[c][:L] + prod[:L]
            hs[p_] = g_key[c] * hs[p_] + prod[L:] + yg[c][L:]
    for p_ in range(half):
        h_ref[p_] = hs[p_]

    y = jnp.concatenate(y_chunks, axis=0)
    inv_n = 1.0 / RWKV_HEAD
    mean = _dot2_rhs(y, head_ones) * inv_n
    yc = y - mean
    var = _bdot(yc * yc, head_ones) * inv_n
    yn = yc * lax.rsqrt(var + GN_EPS) * per_row(gg_ref) + per_row(gb_ref)
    out = yn + bonus_prev
    for i in range(half):
        y_ref[:, i * LANES:(i + 1) * LANES] = out[i * ts:(i + 1) * ts, :LANES]
        y_ref[:, (i + half) * LANES:(i + half + 1) * LANES] = out[i * ts:(i + 1) * ts, LANES:]


def _wkv(r, lw, k, v, a, head_params, seq, ts=256):
    t, d = r.shape
    tiles = t // ts
    rows = (d // MXU_DIM) * ts
    in_spec = pl.BlockSpec((ts, d), lambda n: (jnp.minimum(n, tiles - 1), 0))
    out_spec = pl.BlockSpec((ts, d), lambda n: (jnp.maximum(n - 1, 0), 0))
    stage_f32 = pltpu.VMEM((rows, MXU_DIM), F32)
    stage_bf16 = pltpu.VMEM((rows, MXU_DIM), BF16)
    return pl.pallas_call(
        functools.partial(_wkv_kernel, seq_tiles=seq // ts),
        grid=(tiles + 1,),
        in_specs=[in_spec] * 5 + [_const_spec((1, d))] * len(head_params),
        out_specs=out_spec,
        out_shape=jax.ShapeDtypeStruct((t, d), F32),
        scratch_shapes=[pltpu.VMEM((d // MXU_DIM, RWKV_HEAD, MXU_DIM), F32),
                        stage_f32, stage_bf16, stage_f32, stage_f32, stage_bf16, stage_f32,
                        pltpu.VMEM((rows // WKV_CHUNK, MXU_DIM), F32)],
        compiler_params=_params("arbitrary"),
        name="wkv7_chunked",
    )(r, lw, k, v, a, *head_params)


def kernel(x, ln1_g, ln1_b, ln2_g, ln2_b, ffn_w1, ffn_w2, gm_w_in, gm_vn_g, gm_vn_b, gm_ws, gm_bs, gm_w_out, rw_mu, rw_w_rkv, rw_w0, rw_w1, rw_w2, rw_a0, rw_a1, rw_a2, rw_g1, rw_g2, rw_k_k, rw_k_a, rw_r_k, rw_gn_g, rw_gn_b, rw_w_o, rw_v0, rw_v1, rw_v2):
    batch, seq, d = x.shape
    depth = ln1_g.shape[0]
    alpha = (2.0 * depth) ** 0.25
    h = x.reshape(batch * seq, d)
    row = lambda vec: vec.reshape(1, -1)
    causal = jnp.tril(jnp.ones((GMLP_CHUNK, GMLP_CHUNK), dtype=bool))
    ffn_w1, ffn_w2, gm_w_in, gm_w_out, rw_w_rkv, rw_w_o = (
        w.astype(BF16) for w in (ffn_w1, ffn_w2, gm_w_in, gm_w_out, rw_w_rkv, rw_w_o))
    v_first = None
    for i in range(depth):
        j = i // 2
        ffn_weights = (_pick(ffn_w1, i), _pick(ffn_w2, i), row(ln2_g[i]), row(ln2_b[i]))
        if i % 2 == 0:
            ws = jnp.where(causal[None], gm_ws[j], 0.0).astype(BF16)
            bs = jnp.broadcast_to(gm_bs[j][:, :, None], gm_bs[j].shape + (d // GMLP_GROUPS,))
            h = _gmlp(h, _pick(gm_w_in, j), row(gm_vn_g[j]), row(gm_vn_b[j]), ws, bs,
                      _pick(gm_w_out, j), row(ln1_g[i]), row(ln1_b[i]), alpha)
            h = _ffn(h, *ffn_weights, alpha)
        else:
            vres = None
            if j > 0:
                vres = (v_first, row(rw_v0[j - 1]), rw_v1[j - 1].astype(BF16),
                        rw_v2[j - 1].astype(BF16))
            proj_weights = (rw_mu[j], _pick(rw_w_rkv, j, 0), _pick(rw_w_rkv, j, 1),
                            _pick(rw_w_rkv, j, 2), row(rw_w0[j]), rw_w1[j].astype(BF16),
                            rw_w2[j].astype(BF16), row(rw_a0[j]), rw_a1[j].astype(BF16),
                            rw_a2[j].astype(BF16), rw_g1[j].astype(BF16), rw_g2[j].astype(BF16))
            head_params = (row(rw_k_k[j]), row(rw_k_a[j]), row(rw_r_k[j]), row(rw_gn_g[j]),
                           row(rw_gn_b[j]))
            r, lw, k, v, a, g = _rwkv_proj(h, seq, proj_weights, vres)
            if v_first is None:
                v_first = v
            y = _wkv(r, lw, k, v, a, head_params, seq)
            h = _out_ffn(y, g, h, _pick(rw_w_o, j), row(ln1_g[i]), row(ln1_b[i]), *ffn_weights,
                         alpha)
    return h.reshape(batch, seq, d)
```

```python
import functools
import math

import jax
import jax.numpy as jnp
from jax import lax
from jax.experimental import pallas as pl
from jax.experimental.pallas import tpu as pltpu

F32 = jnp.float32
BF16 = jnp.bfloat16

GMLP_CHUNK = 128
GMLP_GROUPS = 8
RWKV_HEAD = 64
LN_EPS = 1e-5
GN_EPS = 64e-5

LANES = 128
MXU_DIM = 256
WKV_CHUNK = 64
VMEM_LIMIT = 56 * 1024 * 1024


def _bdot(a, b):
    return jnp.dot(a.astype(BF16), b.astype(BF16), preferred_element_type=F32)


def _split3(x):
    hi = x.astype(BF16)
    r1 = x - hi.astype(F32)
    mid = r1.astype(BF16)
    lo = (r1 - mid.astype(F32)).astype(BF16)
    return hi, mid, lo


def _dot_exact_lhs(a_bf16, x):
    hi, mid, lo = _split3(x)
    out = jnp.dot(a_bf16, hi, preferred_element_type=F32)
    out += jnp.dot(a_bf16, mid, preferred_element_type=F32)
    out += jnp.dot(a_bf16, lo, preferred_element_type=F32)
    return out


def _dot2_rhs(x, b_bf16):
    hi = x.astype(BF16)
    lo = (x - hi.astype(F32)).astype(BF16)
    n = x.shape[0]
    both = jnp.dot(jnp.concatenate([hi, lo], axis=0), b_bf16, preferred_element_type=F32)
    return both[:n] + both[n:]


def _layer_norm(x, g, b, eps):
    mean = jnp.mean(x, axis=-1, keepdims=True)
    xc = x - mean
    var = jnp.mean(xc * xc, axis=-1, keepdims=True)
    return xc * lax.rsqrt(var + eps) * g + b


def _const_spec(shape):
    nd = len(shape)
    return pl.BlockSpec(shape, lambda *_: (0,) * nd, pipeline_mode=pl.Buffered(1))


def _pick(stacked, *idx):
    return (stacked, idx)


def _wshape(w):
    return w[0].shape[len(w[1]):] if isinstance(w, tuple) else w.shape


def _warg(w):
    return w[0] if isinstance(w, tuple) else w


def _wspec(w):
    if not isinstance(w, tuple):
        return _const_spec(w.shape)
    idx, rest = w[1], _wshape(w)
    return pl.BlockSpec((None,) * len(idx) + rest, lambda *_: idx + (0,) * len(rest),
                        pipeline_mode=pl.Buffered(1))


def _params(*sem):
    return pltpu.CompilerParams(dimension_semantics=sem, vmem_limit_bytes=VMEM_LIMIT)


def _ffn_body(h, w1_ref, w2_ref, g_ref, b_ref, alpha, ff_block):
    hb = h.astype(BF16)
    d_ff = w1_ref.shape[1]
    acc = alpha * h
    for j in range(d_ff // ff_block):
        a = jnp.dot(hb, w1_ref[:, j * ff_block:(j + 1) * ff_block], preferred_element_type=F32)
        a = jnp.maximum(a, 0.0)
        a = (a * a).astype(BF16)
        acc += jnp.dot(a, w2_ref[j * ff_block:(j + 1) * ff_block, :], preferred_element_type=F32)
    return _layer_norm(acc, g_ref[...], b_ref[...], LN_EPS)


def _ffn_kernel(h_ref, w1_ref, w2_ref, g_ref, b_ref, o_ref, *, alpha, ff_block):
    o_ref[...] = _ffn_body(h_ref[...], w1_ref, w2_ref, g_ref, b_ref, alpha, ff_block)


def _out_ffn_kernel(y_ref, h_ref, wo_ref, l1g_ref, l1b_ref, w1_ref, w2_ref, l2g_ref, l2b_ref,
                    o_ref, *, alpha, ff_block, parts):
    pm = h_ref.shape[0] // parts
    rows = [slice(i * pm, (i + 1) * pm) for i in range(parts)]
    mixes = [jnp.dot(y_ref[r].astype(BF16), wo_ref[...], preferred_element_type=F32) for r in rows]
    h1 = jnp.concatenate([_layer_norm(alpha * h_ref[r] + m, l1g_ref[...], l1b_ref[...], LN_EPS)
                          for r, m in zip(rows, mixes)], axis=0)
    o_ref[...] = _ffn_body(h1, w1_ref, w2_ref, l2g_ref, l2b_ref, alpha, ff_block)


def _out_ffn(y, h, w_o, ln1_g, ln1_b, w1, w2, ln2_g, ln2_b, alpha, tm=1024, ff_block=1024,
             parts=4):
    t, d = h.shape
    row_spec = pl.BlockSpec((tm, d), lambda i: (i, 0))
    return pl.pallas_call(
        functools.partial(_out_ffn_kernel, alpha=alpha, ff_block=ff_block, parts=parts),
        grid=(t // tm,),
        in_specs=[row_spec, row_spec, _wspec(w_o), _const_spec((1, d)),
                  _const_spec((1, d)), _wspec(w1), _wspec(w2),
                  _const_spec((1, d)), _const_spec((1, d))],
        out_specs=row_spec,
        out_shape=jax.ShapeDtypeStruct((t, d), F32),
        compiler_params=_params("parallel"),
        name="rwkv_out_ffn_ln",
    )(y, h, _warg(w_o), ln1_g, ln1_b, _warg(w1), _warg(w2), ln2_g, ln2_b)


def _ffn(h, w1, w2, g, b, alpha, tm=1024, ff_block=1024):
    t, d = h.shape
    return pl.pallas_call(
        functools.partial(_ffn_kernel, alpha=alpha, ff_block=ff_block),
        grid=(t // tm,),
        in_specs=[pl.BlockSpec((tm, d), lambda i: (i, 0)), _wspec(w1), _wspec(w2),
                  _const_spec((1, d)), _const_spec((1, d))],
        out_specs=pl.BlockSpec((tm, d), lambda i: (i, 0)),
        out_shape=jax.ShapeDtypeStruct((t, d), F32),
        compiler_params=_params("parallel"),
        name="ffn_ln",
    )(h, _warg(w1), _warg(w2), g, b)


def _gmlp_kernel(x_ref, win_ref, vng_ref, vnb_ref, ws_ref, bs_ref, wout_ref, g_ref, b_ref, o_ref,
                 *, alpha, parts):
    tm = x_ref.shape[0]
    gw = wout_ref.shape[0]
    gd = gw // GMLP_GROUPS
    pm = tm // parts
    xs = [x_ref[i * pm:(i + 1) * pm] for i in range(parts)]
    zs = [jnp.dot(xs[i].astype(BF16), win_ref[...], preferred_element_type=F32)
          for i in range(parts)]
    for i in range(parts):
        z = zs[i]
        z = 0.5 * z * (1.0 + lax.erf(z * math.sqrt(0.5)))
        u = z[:, :gw]
        v = _layer_norm(z[:, gw:], vng_ref[...], vnb_ref[...], LN_EPS).astype(BF16)
        rows = []
        for c in range(pm // GMLP_CHUNK):
            cols = []
            for g in range(GMLP_GROUPS):
                vb = v[c * GMLP_CHUNK:(c + 1) * GMLP_CHUNK, g * gd:(g + 1) * gd]
                cols.append(jnp.dot(ws_ref[g], vb, preferred_element_type=F32) + bs_ref[g])
            rows.append(jnp.concatenate(cols, axis=1))
        mixed = jnp.concatenate(rows, axis=0)
        mix = jnp.dot((u * mixed).astype(BF16), wout_ref[...], preferred_element_type=F32)
        o_ref[i * pm:(i + 1) * pm] = _layer_norm(alpha * xs[i] + mix, g_ref[...], b_ref[...],
                                                  LN_EPS)


def _gmlp(x, w_in, vn_g, vn_b, ws, bs, w_out, g, b, alpha, tm=1024, parts=4):
    t, d = x.shape
    gw = _wshape(w_out)[0]
    return pl.pallas_call(
        functools.partial(_gmlp_kernel, alpha=alpha, parts=parts),
        grid=(t // tm,),
        in_specs=[pl.BlockSpec((tm, d), lambda i: (i, 0)),
                  _wspec(w_in), _const_spec((1, gw)), _const_spec((1, gw)),
                  _const_spec(ws.shape), _const_spec(bs.shape), _wspec(w_out),
                  _const_spec((1, d)), _const_spec((1, d))],
        out_specs=pl.BlockSpec((tm, d), lambda i: (i, 0)),
        out_shape=jax.ShapeDtypeStruct((t, d), F32),
        compiler_params=_params("parallel"),
        name="gmlp_ln",
    )(x, _warg(w_in), vn_g, vn_b, ws, bs, _warg(w_out), g, b)


def _rwkv_proj_kernel(*refs, seq_tiles, has_vres):
    if has_vres:
        (x_ref, xp_ref, mu_ref, wr_ref, wk_ref, wv_ref, w0_ref, w1_ref, w2_ref, a0_ref, a1_ref,
         a2_ref, g1_ref, g2_ref, vf_ref, v0_ref, v1_ref, v2_ref,
         r_out, lw_out, k_out, v_out, a_out, g_out) = refs
    else:
        (x_ref, xp_ref, mu_ref, wr_ref, wk_ref, wv_ref, w0_ref, w1_ref, w2_ref, a0_ref, a1_ref,
         a2_ref, g1_ref, g2_ref,
         r_out, lw_out, k_out, v_out, a_out, g_out) = refs
    x = x_ref[...]
    not_first = (pl.program_id(0) % seq_tiles != 0).astype(F32)
    prev = xp_ref[7:8, :] * not_first
    row = lax.broadcasted_iota(jnp.int32, x.shape, 0)
    shifted = jnp.where(row == 0, prev, pltpu.roll(x, shift=1, axis=0))
    xx = shifted - x
    mix = lambda i: (x + xx * mu_ref[i:i + 1, :]).astype(BF16)

    r_out[...] = _bdot(mix(0), wr_ref[...])
    w_in = w0_ref[...] + _bdot(jnp.tanh(_bdot(mix(1), w1_ref[...])), w2_ref[...])
    k_out[...] = _bdot(mix(2), wk_ref[...])
    lw_out[...] = -math.exp(-0.5) * jax.nn.sigmoid(w_in)
    xv = mix(3)
    v = _bdot(xv, wv_ref[...])
    a_in = a0_ref[...] + _bdot(_bdot(mix(4), a1_ref[...]), a2_ref[...])
    if has_vres:
        gate_in = v0_ref[...] + _bdot(_bdot(xv, v1_ref[...]), v2_ref[...])
    g_hidden = _bdot(mix(5), g1_ref[...])
    a_out[...] = jax.nn.sigmoid(a_in)
    if has_vres:
        v = v + (vf_ref[...] - v) * jax.nn.sigmoid(gate_in)
    v_out[...] = v
    g_out[...] = _bdot(jax.nn.sigmoid(g_hidden), g2_ref[...])


def _rwkv_proj(h, seq, weights, vres, tm=512):
    t, d = h.shape
    has_vres = vres is not None
    row_spec = pl.BlockSpec((tm, d), lambda i: (i, 0))
    prev_spec = pl.BlockSpec((8, d), lambda i: (jnp.maximum(i * (tm // 8) - 1, 0), 0))
    args = [h, h] + [_warg(w) for w in weights]
    in_specs = [row_spec, prev_spec] + [_wspec(w) for w in weights]
    if has_vres:
        v_first, v0, v1, v2 = vres
        args += [v_first, v0, v1, v2]
        in_specs += [row_spec, _const_spec(v0.shape), _const_spec(v1.shape), _const_spec(v2.shape)]
    return pl.pallas_call(
        functools.partial(_rwkv_proj_kernel, seq_tiles=seq // tm, has_vres=has_vres),
        grid=(t // tm,),
        in_specs=in_specs,
        out_specs=[row_spec] * 6,
        out_shape=[jax.ShapeDtypeStruct((t, d), F32)] * 6,
        compiler_params=_params("parallel"),
        name="rwkv_proj",
    )(*args)


def _quarters(x, quarter_masks):
    zero = jnp.zeros_like(x)
    return jnp.concatenate([jnp.where(m, x, zero) for m in quarter_masks], axis=0)


def _head_transpose(x, lo_mask):
    halves = []
    for s in range(x.shape[1] // LANES):
        xs = x[:, s * LANES:(s + 1) * LANES]
        t = jnp.concatenate([xs, xs], axis=0).T
        halves.append(jnp.where(lo_mask, t[:RWKV_HEAD], t[RWKV_HEAD:]))
    return jnp.concatenate(halves, axis=1)


def _wkv_kernel(r_ref, lw_ref, k_ref, v_ref, a_ref, gate_ref, kk_ref, ka_ref, rk_ref, gg_ref, gb_ref,
                y_ref, h_ref, rt_s, nkt_s, bt_s, kk_s, vb_s, bonus_s, glast_s, *, seq_tiles):
    ts, d = r_ref.shape
    pairs = d // LANES
    half = pairs // 2
    rows = half * ts
    W = MXU_DIM
    L = WKV_CHUNK
    per_pair = ts // L
    n_chunks = half * per_pair
    step = pl.program_id(0)

    @pl.when(step == 0)
    def _():
        for ref in (h_ref, rt_s, nkt_s, bt_s, kk_s, vb_s, bonus_s, glast_s):
            ref[...] = jnp.zeros_like(ref)

    def slot_cols(x, i):
        return jnp.concatenate([x[:, i * LANES:(i + 1) * LANES],
                                x[:, (i + half) * LANES:(i + half + 1) * LANES]], axis=1)

    def per_row(ref):
        p = ref[...]
        return jnp.concatenate([jnp.broadcast_to(slot_cols(p, i), (ts, W)) for i in range(half)],
                               axis=0)

    li = lax.broadcasted_iota(jnp.int32, (L, W), 0)
    lj = lax.broadcasted_iota(jnp.int32, (L, W), 1)
    quarter_masks = [(lj // RWKV_HEAD) == q for q in range(W // RWKV_HEAD)]
    src = lj % RWKV_HEAD
    strict = li > src
    incl = li >= src
    eye = (li == src).astype(F32)
    lo_mask = lax.broadcasted_iota(jnp.int32, (L, LANES), 1) < RWKV_HEAD
    qi = lax.broadcasted_iota(jnp.int32, (W, W), 0)
    qj = lax.broadcasted_iota(jnp.int32, (W, W), 1)
    head_ones = ((qi // RWKV_HEAD) == (qj // RWKV_HEAD)).astype(BF16)
    ti = lax.broadcasted_iota(jnp.int32, (LANES, LANES), 0)
    tj = lax.broadcasted_iota(jnp.int32, (LANES, LANES), 1)
    tri = ((ti >= tj) & (ti // L == tj // L)).astype(BF16)

    chunks = range(n_chunks)
    bd = lambda x: _quarters(x.astype(BF16), quarter_masks)
    mm = lambda x, w: jnp.dot(x.astype(BF16), w, preferred_element_type=F32)
    bonus_prev = bonus_s[...]
    rt = [rt_s[c * L:(c + 1) * L] for c in chunks]
    nkt = [nkt_s[c * L:(c + 1) * L] for c in chunks]
    bt = [bt_s[c * L:(c + 1) * L] for c in chunks]
    kk_ = [kk_s[c * L:(c + 1) * L] for c in chunks]
    bd_v = [bd(vb_s[c * L:(c + 1) * L]) for c in chunks]
    g_last = [glast_s[c:c + 1, :] for c in chunks]

    att = [lax.dot_general(jnp.concatenate([nkt[c], rt[c].astype(BF16)], axis=0),
                           jnp.concatenate([bd(bt[c]), bd(kk_[c])], axis=0),
                           (((1,), (1,)), ((), ())), preferred_element_type=F32) for c in chunks]
    q = [jnp.where(strict, att[c][:L, :W], 0.0) for c in chunks]
    nak = [jnp.where(strict, att[c][:L, W:], 0.0).astype(BF16) for c in chunks]
    a_rb = [jnp.where(incl, att[c][L:, :W], 0.0).astype(BF16) for c in chunks]
    a_rk = [jnp.where(incl, att[c][L:, W:], 0.0).astype(BF16) for c in chunks]
    nav = [mm(nak[c], bd_v[c]) for c in chunks]
    be_t = [_head_transpose(bt[c] * g_last[c], lo_mask).astype(BF16) for c in chunks]
    ke_t = [_head_transpose(kk_[c] * g_last[c], lo_mask).astype(BF16) for c in chunks]
    g_key = []
    for c in chunks:
        sides = []
        for s in range(W // LANES):
            t = jnp.broadcast_to(g_last[c][:, s * LANES:(s + 1) * LANES], (LANES, LANES)).T
            sides.append(jnp.where(lo_mask, t[:RWKV_HEAD], t[RWKV_HEAD:]))
        g_key.append(jnp.concatenate(sides, axis=1))

    def prepare(i):
        slot = lambda ref: jnp.concatenate(
            [ref[:, i * LANES:(i + 1) * LANES],
             ref[:, (i + half) * LANES:(i + half + 1) * LANES]], axis=1)
        sl = slice(i * ts, (i + 1) * ts)
        r, lw, k, v, a = slot(r_ref), slot(lw_ref), slot(k_ref), slot(v_ref), slot(a_ref)
        kraw = k * slot(kk_ref)
        k2 = k * (1.0 + (a - 1.0) * slot(ka_ref))
        sums = _bdot(jnp.concatenate([kraw * kraw, r * k2 * slot(rk_ref)], axis=0), head_ones)
        kappa = kraw / jnp.maximum(jnp.sqrt(sums[:ts]), 1e-12)
        cs = jnp.concatenate([_dot_exact_lhs(tri, lw[j * LANES:(j + 1) * LANES])
                              for j in range(ts // LANES)], axis=0)
        g_incl = jnp.exp(cs)
        g_inv = jnp.exp(-cs)
        rt_s[sl] = r * g_incl
        nkt_s[sl] = (-kappa * jnp.exp(cs - lw)).astype(BF16)
        bt_s[sl] = kappa * a * g_inv
        kk_s[sl] = k2 * g_inv
        vb_s[sl] = v.astype(BF16)
        bonus_s[sl] = sums[ts:] * v
        glast_s[i * per_pair:(i + 1) * per_pair] = jnp.concatenate(
            [g_incl[(j + 1) * L - 1:(j + 1) * L, :] for j in range(per_pair)], axis=0)

    p = [eye + q[c] for c in chunks]
    qb = [q[c].astype(BF16) for c in chunks]
    q = [mm(qb[c], bd(qb[c])) for c in chunks]
    doublings = int(math.log2(L)) - 2
    assert half == doublings, "one prepare() slot per doubling step"
    for i in range(doublings):
        prepare(i)
        qb = [q[c].astype(BF16) for c in chunks]
        pq = [mm(jnp.concatenate([qb[c], p[c].astype(BF16)], axis=0), bd(qb[c])) for c in chunks]
        q = [pq[c][:L] for c in chunks]
        p = [p[c] + pq[c][L:] for c in chunks]
    p = [p[c] + mm(p[c], bd(q[c])) for c in chunks]
    tw = [mm(p[c], jnp.concatenate([bd(nkt[c]), bd(nav[c])], axis=1)) for c in chunks]
    nw = [bd(tw[c][:, :W]) for c in chunks]
    bd_uv = [jnp.concatenate([bd(tw[c][:, W:]), bd_v[c]], axis=0) for c in chunks]
    yg = [mm(jnp.concatenate([jnp.concatenate([a_rb[c], a_rk[c]], axis=1),
                              jnp.concatenate([be_t[c], ke_t[c]], axis=1)], axis=0), bd_uv[c])
          for c in chunks]
    rm = [mm(jnp.concatenate([a_rb[c], be_t[c]], axis=0), nw[c]) for c in chunks]
    lhs = [jnp.concatenate([rt[c] + rm[c][:L], rm[c][L:]], axis=0).astype(BF16) for c in chunks]

    seq_start = (step + seq_tiles - 1) % seq_tiles == 0
    hs = [jnp.where(seq_start, 0.0, h_ref[p_]) for p_ in range(half)]
    y_chunks = [None] * n_chunks
    for j in range(per_pair):
        for p_ in range(half):
            c = p_ * per_pair + j
            prod = mm(lhs[c], bd(hs[p_]))
            y_chunks[c] = yg[c][:L] + prod[:L]
            hs[p_] = g_key[c] * hs[p_] + prod[L:] + yg[c][L:]
    for p_ in range(half):
        h_ref[p_] = hs[p_]

    y = jnp.concatenate(y_chunks, axis=0)
    inv_n = 1.0 / RWKV_HEAD
    mean = _dot2_rhs(y, head_ones) * inv_n
    yc = y - mean
    var = _bdot(yc * yc, head_ones) * inv_n
    yn = yc * lax.rsqrt(var + GN_EPS) * per_row(gg_ref) + per_row(gb_ref)
    out = yn + bonus_prev
    for i in range(half):
        lo, hi = slice(i * LANES, (i + 1) * LANES), slice((i + half) * LANES, (i + half + 1) * LANES)
        y_ref[:, lo] = out[i * ts:(i + 1) * ts, :LANES] * gate_ref[:, lo]
        y_ref[:, hi] = out[i * ts:(i + 1) * ts, LANES:] * gate_ref[:, hi]


def _wkv(r, lw, k, v, a, gate, head_params, seq, ts=256):
    t, d = r.shape
    tiles = t // ts
    rows = (d // MXU_DIM) * ts
    in_spec = pl.BlockSpec((ts, d), lambda n: (jnp.minimum(n, tiles - 1), 0))
    out_spec = pl.BlockSpec((ts, d), lambda n: (jnp.maximum(n - 1, 0), 0))
    stage_f32 = pltpu.VMEM((rows, MXU_DIM), F32)
    stage_bf16 = pltpu.VMEM((rows, MXU_DIM), BF16)
    return pl.pallas_call(
        functools.partial(_wkv_kernel, seq_tiles=seq // ts),
        grid=(tiles + 1,),
        in_specs=[in_spec] * 5 + [out_spec] + [_const_spec((1, d))] * len(head_params),
        out_specs=out_spec,
        out_shape=jax.ShapeDtypeStruct((t, d), F32),
        scratch_shapes=[pltpu.VMEM((d // MXU_DIM, RWKV_HEAD, MXU_DIM), F32),
                        stage_f32, stage_bf16, stage_f32, stage_f32, stage_bf16, stage_f32,
                        pltpu.VMEM((rows // WKV_CHUNK, MXU_DIM), F32)],
        compiler_params=_params("arbitrary"),
        name="wkv7_chunked",
    )(r, lw, k, v, a, gate, *head_params)


def kernel(x, ln1_g, ln1_b, ln2_g, ln2_b, ffn_w1, ffn_w2, gm_w_in, gm_vn_g, gm_vn_b, gm_ws, gm_bs, gm_w_out, rw_mu, rw_w_rkv, rw_w0, rw_w1, rw_w2, rw_a0, rw_a1, rw_a2, rw_g1, rw_g2, rw_k_k, rw_k_a, rw_r_k, rw_gn_g, rw_gn_b, rw_w_o, rw_v0, rw_v1, rw_v2):
    batch, seq, d = x.shape
    depth = ln1_g.shape[0]
    alpha = (2.0 * depth) ** 0.25
    h = x.reshape(batch * seq, d)
    row = lambda vec: vec.reshape(1, -1)
    causal = jnp.tril(jnp.ones((GMLP_CHUNK, GMLP_CHUNK), dtype=bool))
    ffn_w1, ffn_w2, gm_w_in, gm_w_out, rw_w_rkv, rw_w_o = (
        w.astype(BF16) for w in (ffn_w1, ffn_w2, gm_w_in, gm_w_out, rw_w_rkv, rw_w_o))
    v_first = None
    for i in range(depth):
        j = i // 2
        ffn_weights = (_pick(ffn_w1, i), _pick(ffn_w2, i), row(ln2_g[i]), row(ln2_b[i]))
        if i % 2 == 0:
            ws = jnp.where(causal[None], gm_ws[j], 0.0).astype(BF16)
            bs = jnp.broadcast_to(gm_bs[j][:, :, None], gm_bs[j].shape + (d // GMLP_GROUPS,))
            h = _gmlp(h, _pick(gm_w_in, j), row(gm_vn_g[j]), row(gm_vn_b[j]), ws, bs,
                      _pick(gm_w_out, j), row(ln1_g[i]), row(ln1_b[i]), alpha)
            h = _ffn(h, *ffn_weights, alpha)
        else:
            vres = None
            if j > 0:
                vres = (v_first, row(rw_v0[j - 1]), rw_v1[j - 1].astype(BF16),
                        rw_v2[j - 1].astype(BF16))
            proj_weights = (rw_mu[j], _pick(rw_w_rkv, j, 0), _pick(rw_w_rkv, j, 1),
                            _pick(rw_w_rkv, j, 2), row(rw_w0[j]), rw_w1[j].astype(BF16),
                            rw_w2[j].astype(BF16), row(rw_a0[j]), rw_a1[j].astype(BF16),
                            rw_a2[j].astype(BF16), rw_g1[j].astype(BF16), rw_g2[j].astype(BF16))
            head_params = (row(rw_k_k[j]), row(rw_k_a[j]), row(rw_r_k[j]), row(rw_gn_g[j]),
                           row(rw_gn_b[j]))
            r, lw, k, v, a, g = _rwkv_proj(h, seq, proj_weights, vres)
            if v_first is None:
                v_first = v
            y = _wkv(r, lw, k, v, a, g, head_params, seq)
            h = _out_ffn(y, h, _pick(rw_w_o, j), row(ln1_g[i]), row(ln1_b[i]), *ffn_weights,
                         alpha)
    return h.reshape(batch, seq, d)
```
